```python
import jax, jax.numpy as jnp
from jax import lax
import numpy as np

D_MODEL = 1024
BATCH = 8
SEQ = 8192
DEPTH = 1

HEAD_DIM = 64
N_ATTN_HEADS = 8
ATTN_WIDTH = N_ATTN_HEADS * HEAD_DIM
N_CONV_GROUPS = 8
CONV_WIDTH = D_MODEL - ATTN_WIDTH
CONV_K = 3
IN_WIDTH = 3 * ATTN_WIDTH + 3 * CONV_WIDTH
DILATED_BRANCHES = ((128, 1), (512, 4), (2048, 16))
MAX_HALF = max(w // 2 for w, _ in DILATED_BRANCHES)
Q_BLOCK = 128
N_EXPERTS = 32
TOP_K = 4
D_FF = D_MODEL
SWIGLU_ALPHA = 1.702
SWIGLU_LIMIT = 7.0
EXPERT_BLOCK = 256
RMS_EPS = 1e-5
NEG_INF = -1e30

kernel_name = "hybrid_dilated_attn_shortconv_moe_encoder"


def rmsnorm(x, g):
    xf = x.astype(jnp.float32)
    y = xf * lax.rsqrt(jnp.mean(xf * xf, axis=-1, keepdims=True) + RMS_EPS)
    return (y * g.astype(jnp.float32)).astype(g.dtype)


def alibi_slopes(n_heads):
    return jnp.exp2(-8.0 * jnp.arange(1, n_heads + 1, dtype=jnp.float32) / n_heads)


def dilated_branch(qb, kp, vp, mp, q0, half, dil, slopes):
    B, H, QB, Dh = qb.shape
    n = half // dil
    L = QB + 2 * half
    A, C = QB // dil, L // dil
    start = q0 + (MAX_HALF - half)
    kb = lax.dynamic_slice_in_dim(kp, start, L, axis=2).reshape(B, H, C, dil, Dh)
    vb = lax.dynamic_slice_in_dim(vp, start, L, axis=2).reshape(B, H, C, dil, Dh)
    mb = lax.dynamic_slice_in_dim(mp, start, L, axis=0).reshape(C, dil).T
    qr = qb.reshape(B, H, A, dil, Dh)
    s = jnp.einsum('bhard,bhcrd->bhrac', qr, kb).astype(jnp.float32) * (HEAD_DIM ** -0.5)
    rel = jnp.arange(C)[None, :] - jnp.arange(A)[:, None] - n
    dist = (jnp.abs(rel) * dil).astype(jnp.float32)
    valid = (jnp.abs(rel) <= n)[None] & mb[:, None, :]
    s = jnp.where(valid, s - slopes[:, None, None, None] * dist, NEG_INF)
    m = jnp.max(s, axis=-1, keepdims=True)
    p = jnp.exp(s - m)
    den = jnp.sum(p, axis=-1)
    den_t = den.transpose(0, 1, 3, 2)
    o = jnp.einsum('bhrac,bhcrd->bhard', p, vb.astype(jnp.float32)) / den_t[..., None]
    lse = (m[..., 0].transpose(0, 1, 3, 2) + jnp.log(den_t)).reshape(B, H, QB)
    return o.reshape(B, H, QB, Dh), lse


def dilated_attention(q, k, v, slopes):
    B, H, S, Dh = q.shape
    pad = ((0, 0), (0, 0), (MAX_HALF, MAX_HALF), (0, 0))
    kp = jnp.pad(k, pad)
    vp = jnp.pad(v, pad)
    mp = jnp.pad(jnp.ones((S,), dtype=bool), MAX_HALF)

    def block(q0):
        qb = lax.dynamic_slice_in_dim(q, q0, Q_BLOCK, axis=2)
        outs, lses = [], []
        for window, dil in DILATED_BRANCHES:
            o, lse = dilated_branch(qb, kp, vp, mp, q0, window // 2, dil, slopes)
            outs.append(o)
            lses.append(lse)
        w = jax.nn.softmax(jnp.stack(lses), axis=0)
        return jnp.einsum('gbhq,gbhqd->bhqd', w, jnp.stack(outs))

    q0s = jnp.arange(S // Q_BLOCK, dtype=jnp.int32) * Q_BLOCK
    o = lax.map(block, q0s)
    return o.transpose(1, 0, 3, 2, 4).reshape(B, S, H * Dh)


def short_gated_conv(gb, gc, u, conv_w):
    z = gc * u
    zc = lax.conv_general_dilated(
        z, conv_w[:, None, :], window_strides=(1,),
        padding=((CONV_K // 2, CONV_K // 2),),
        dimension_numbers=('NWC', 'WIO', 'NWC'),
        feature_group_count=CONV_WIDTH)
    return gb * zc


def moe_ffn(h, w_router, b_router, w_gate_up, b_gate_up, w_down, b_down):
    B, S, D = h.shape
    T = B * S
    xf = h.reshape(T, D)
    logits = (xf @ w_router).astype(jnp.float32) + b_router.astype(jnp.float32)
    top_val, top_idx = lax.top_k(logits, TOP_K)
    gates = jax.nn.softmax(top_val, axis=-1)
    TK = T * TOP_K
    e_flat = top_idx.reshape(TK)
    tok_flat = jnp.arange(TK, dtype=jnp.int32) // TOP_K
    g_flat = gates.reshape(TK)
    order = jnp.argsort(e_flat)
    e_s, tok_s, g_s = e_flat[order], tok_flat[order], g_flat[order]
    counts = jnp.bincount(e_flat, length=N_EXPERTS)
    padded = (counts + EXPERT_BLOCK - 1) // EXPERT_BLOCK * EXPERT_BLOCK
    pad_end = jnp.cumsum(padded)
    pad_start = pad_end - padded
    grp_start = jnp.cumsum(counts) - counts
    dest = pad_start[e_s] + jnp.arange(TK, dtype=jnp.int32) - grp_start[e_s]
    n_blocks = -(-TK // EXPERT_BLOCK) + N_EXPERTS
    P = n_blocks * EXPERT_BLOCK
    buf_tok = jnp.zeros((P,), jnp.int32).at[dest].set(tok_s)
    buf_gate = jnp.zeros((P,), jnp.float32).at[dest].set(g_s)
    blk_exp = jnp.clip(jnp.searchsorted(pad_end, jnp.arange(n_blocks) * EXPERT_BLOCK,
                                        side='right'), 0, N_EXPERTS - 1)

    def expert_block(args):
        tok, g, e = args
        xb = xf[tok]
        hu = xb @ w_gate_up[e] + b_gate_up[e]
        a = jnp.minimum(hu[:, :D_FF], SWIGLU_LIMIT)
        lin = jnp.clip(hu[:, D_FF:], -SWIGLU_LIMIT, SWIGLU_LIMIT)
        act = a * jax.nn.sigmoid(SWIGLU_ALPHA * a) * (lin + 1)
        return (act @ w_down[e] + b_down[e]) * g[:, None].astype(xb.dtype)

    out = lax.map(expert_block, (buf_tok.reshape(n_blocks, EXPERT_BLOCK),
                                 buf_gate.reshape(n_blocks, EXPERT_BLOCK), blk_exp))
    y = jax.ops.segment_sum(out.reshape(P, D), buf_tok, num_segments=T)
    return y.reshape(B, S, D).astype(h.dtype)


def setup_inputs(seed: int = 0) -> dict:
    key = jax.random.key(seed)
    ks = jax.random.split(key, 16)
    f32 = jnp.float32

    def gain(k, n):
        return jnp.ones((DEPTH, n), f32) + 0.02 * jax.random.normal(k, (DEPTH, n), f32)

    return {
        "x": jax.random.normal(ks[0], (BATCH, SEQ, D_MODEL), f32),
        "mix_norm_g": gain(ks[1], D_MODEL),
        "w_in": jax.random.normal(ks[2], (DEPTH, D_MODEL, IN_WIDTH), f32) * D_MODEL ** -0.5,
        "conv_w": jax.random.normal(ks[3], (DEPTH, CONV_K, CONV_WIDTH), f32) * CONV_K ** -0.5,
        "attn_out_norm_g": gain(ks[4], ATTN_WIDTH),
        "conv_out_norm_g": gain(ks[5], CONV_WIDTH),
        "w_out": jax.random.normal(ks[6], (DEPTH, D_MODEL, D_MODEL), f32) * D_MODEL ** -0.5,
        "ffn_norm_g": gain(ks[7], D_MODEL),
        "w_router": jax.random.normal(ks[8], (DEPTH, D_MODEL, N_EXPERTS), f32) * D_MODEL ** -0.5,
        "b_router": 0.01 * jax.random.normal(ks[9], (DEPTH, N_EXPERTS), f32),
        "w_gate_up": jax.random.normal(ks[10], (DEPTH, N_EXPERTS, D_MODEL, 2 * D_FF), f32) * D_MODEL ** -0.5,
        "b_gate_up": 0.01 * jax.random.normal(ks[11], (DEPTH, N_EXPERTS, 2 * D_FF), f32),
        "w_down": jax.random.normal(ks[12], (DEPTH, N_EXPERTS, D_FF, D_MODEL), f32) * D_FF ** -0.5,
        "b_down": 0.01 * jax.random.normal(ks[13], (DEPTH, N_EXPERTS, D_MODEL), f32),
        "final_norm_g": jnp.ones((D_MODEL,), f32) + 0.02 * jax.random.normal(ks[14], (D_MODEL,), f32),
    }


def reference(x, mix_norm_g, w_in, conv_w, attn_out_norm_g, conv_out_norm_g, w_out,
              ffn_norm_g, w_router, b_router, w_gate_up, b_gate_up, w_down, b_down,
              final_norm_g):
    B, S, _ = x.shape
    slopes = alibi_slopes(N_ATTN_HEADS)
    splits = [int(s) for s in np.cumsum([ATTN_WIDTH] * 3 + [CONV_WIDTH] * 2)]
    for l in range(DEPTH):
        h = rmsnorm(x, mix_norm_g[l])
        proj = h @ w_in[l]
        q, k, v, gb, gc, u = jnp.split(proj, splits, axis=-1)

        def heads(t):
            return t.reshape(B, S, N_ATTN_HEADS, HEAD_DIM).transpose(0, 2, 1, 3)

        attn = dilated_attention(heads(q), heads(k), heads(v), slopes)
        conv = short_gated_conv(gb, gc, u, conv_w[l])
        merged = jnp.concatenate([rmsnorm(attn, attn_out_norm_g[l]),
                                  rmsnorm(conv, conv_out_norm_g[l])], axis=-1)
        x = x + merged @ w_out[l]
        x = x + moe_ffn(rmsnorm(x, ffn_norm_g[l]), w_router[l], b_router[l],
                        w_gate_up[l], b_gate_up[l], w_down[l], b_down[l])
    return rmsnorm(x, final_norm_g)
```

```python
import functools

import numpy as np
import jax
import jax.numpy as jnp
from jax import lax
from jax.experimental import pallas as pl
from jax.experimental.pallas import tpu as pltpu

F32 = jnp.float32
BF16 = jnp.bfloat16
I32 = jnp.int32

HEAD_DIM = 64
N_HEADS = 8
ATTN_W = N_HEADS * HEAD_DIM
CONV_W = 512
N_EXPERTS = 32
TOP_K = 4
BRANCHES = ((128, 1), (512, 4), (2048, 16))
SIDE = 64
SWIGLU_ALPHA = 1.702
SWIGLU_LIMIT = 7.0
RMS_EPS = 1e-5
NEG = -1e30

LANES = 128
Q_SUB = 128
K_WIN = Q_SUB + 2 * SIDE
VMEM_LIMIT = 56 * 1024 * 1024


def _cparams(*sem):
    return pltpu.CompilerParams(dimension_semantics=sem, vmem_limit_bytes=VMEM_LIMIT)


def _rms(x, g):
    return x * lax.rsqrt(jnp.mean(x * x, axis=-1, keepdims=True) + RMS_EPS) * g


def _proj_kernel(x_ref, g_ref, w_ref, q_ref, k_ref, v_ref, gb_ref, z_ref):
    h = _rms(x_ref[...], g_ref[...]).astype(BF16)

    def proj(j):
        return jnp.dot(h, w_ref[:, j * ATTN_W:(j + 1) * ATTN_W], preferred_element_type=F32)

    q_ref[...] = (proj(0) * (HEAD_DIM ** -0.5)).astype(BF16)
    k_ref[...] = proj(1).astype(BF16)
    v_ref[...] = proj(2).astype(BF16)
    gb_ref[...] = proj(3).astype(BF16)
    z_ref[...] = (proj(4) * proj(5)).astype(BF16)


def _proj(x2, g, w_in_bf, tm):
    t, d = x2.shape
    out = jax.ShapeDtypeStruct((t, ATTN_W), BF16)
    row = pl.BlockSpec((tm, ATTN_W), lambda i: (i, 0))
    return pl.pallas_call(
        _proj_kernel,
        grid=(t // tm,),
        in_specs=[pl.BlockSpec((tm, d), lambda i: (i, 0)),
                  pl.BlockSpec((1, d), lambda i: (0, 0)),
                  pl.BlockSpec(w_in_bf.shape, lambda i: (0, 0))],
        out_specs=[row] * 5,
        out_shape=[out] * 5,
        compiler_params=_cparams("parallel"),
        name="proj",
    )(x2, g, w_in_bf)


def _bias_tables(dil):
    row = np.arange(Q_SUB)[:, None]
    col = np.arange(K_WIN)[None, :]
    rel = col - SIDE - row
    band = np.abs(rel) <= SIDE
    slopes = 2.0 ** (-8.0 * np.arange(1, N_HEADS + 1) / N_HEADS)
    tabs = []
    for t in range(4):
        ok = band.copy()
        if t & 1:
            ok &= col >= SIDE
        if t & 2:
            ok &= col < SIDE + Q_SUB
        for h in range(N_HEADS):
            tabs.append(np.where(ok, -slopes[h] * np.abs(rel) * dil, NEG))
    return jnp.asarray(np.stack(tabs), dtype=F32)


def _attn_kernel(q_ref, kp_ref, km_ref, kn_ref, vp_ref, vm_ref, vn_ref, bias_ref,
                 o_ref, lse_ref, kbuf, vbuf, *, tq, n_sub_total):
    i = pl.program_id(2)
    kbuf[0:SIDE] = kp_ref[0]
    kbuf[SIDE:SIDE + tq] = km_ref[0]
    kbuf[SIDE + tq:] = kn_ref[0]
    vbuf[0:SIDE] = vp_ref[0]
    vbuf[SIDE:SIDE + tq] = vm_ref[0]
    vbuf[SIDE + tq:] = vn_ref[0]

    lane = lax.broadcasted_iota(I32, (Q_SUB, LANES), 1)
    lo = lane < HEAD_DIM
    grp = lane // 16
    n_sub = tq // Q_SUB

    def sub(j, carry):
        r0 = pl.multiple_of(j * Q_SUB, Q_SUB)
        g_sub = i * n_sub + j
        tbl = (g_sub == 0).astype(I32) + 2 * (g_sub == n_sub_total - 1).astype(I32)
        lse_tile = jnp.zeros((Q_SUB, LANES), F32)
        for hp in range(N_HEADS // 2):
            cols = slice(hp * LANES, (hp + 1) * LANES)
            qp = q_ref[0, pl.ds(r0, Q_SUB), cols]
            kw = kbuf[pl.ds(r0, K_WIN), cols]
            vw = vbuf[pl.ds(r0, K_WIN), cols]
            outs = []
            for par in range(2):
                qm = jnp.where(lo if par == 0 else jnp.logical_not(lo), qp, jnp.zeros_like(qp))
                s = lax.dot_general(qm, kw, (((1,), (1,)), ((), ())), preferred_element_type=F32)
                s = s + bias_ref[tbl * N_HEADS + 2 * hp + par]
                m = jnp.max(s, axis=1, keepdims=True)
                p = jnp.exp(s - m)
                l = jnp.sum(p, axis=1, keepdims=True)
                pv = jnp.dot(p.astype(BF16), vw, preferred_element_type=F32)
                outs.append(pv / l)
                lse_h = m + jnp.log(l)
                lse_tile = jnp.where(grp == 2 * hp + par, lse_h, lse_tile)
            o_ref[0, pl.ds(r0, Q_SUB), cols] = jnp.where(lo, outs[0], outs[1]).astype(BF16)
        lse_ref[0, pl.ds(r0, Q_SUB), :] = lse_tile
        return carry

    lax.fori_loop(0, n_sub, sub, 0)


def _attn_branch(q, k, v, dil, batch, seq):
    sd = seq // dil
    tq = min(512, sd)
    n_tiles = sd // tq
    qv, kv, vv = (a.reshape(batch, sd, dil * ATTN_W) for a in (q, k, v))
    bias = _bias_tables(dil)
    r_side = tq // SIDE
    last_side = sd // SIDE - 1

    main = pl.BlockSpec((1, tq, ATTN_W), lambda b, r, i: (b, i, r))
    prev = pl.BlockSpec((1, SIDE, ATTN_W), lambda b, r, i: (b, jnp.maximum(i * r_side - 1, 0), r))
    nxt = pl.BlockSpec((1, SIDE, ATTN_W), lambda b, r, i: (b, jnp.minimum((i + 1) * r_side, last_side), r))
    lse_spec = pl.BlockSpec((1, tq, LANES), lambda b, r, i: (b, i, r))
    o, lse = pl.pallas_call(
        functools.partial(_attn_kernel, tq=tq, n_sub_total=sd // Q_SUB),
        grid=(batch, dil, n_tiles),
        in_specs=[main, prev, main, nxt, prev, main, nxt,
                  pl.BlockSpec(bias.shape, lambda b, r, i: (0, 0, 0))],
        out_specs=[main, lse_spec],
        out_shape=[jax.ShapeDtypeStruct((batch, sd, dil * ATTN_W), BF16),
                   jax.ShapeDtypeStruct((batch, sd, dil * LANES), F32)],
        scratch_shapes=[pltpu.VMEM((tq + 2 * SIDE, ATTN_W), BF16),
                        pltpu.VMEM((tq + 2 * SIDE, ATTN_W), BF16)],
        compiler_params=_cparams("parallel", "parallel", "parallel"),
        name=f"attn_d{dil}",
    )(qv, kv, kv, kv, vv, vv, vv, bias)
    return o.reshape(batch * seq, ATTN_W), lse.reshape(batch * seq, LANES)


def _mix_kernel(x_ref, o1_ref, o2_ref, o3_ref, l1_ref, l2_ref, l3_ref, gb_ref, z_ref, zp_ref, zn_ref,
                cw_ref, ga_ref, gc_ref, wo_ref, gf_ref, wr_ref, br_ref, tri_ref,
                x1_ref, h2_ref, ri_ref, rg_ref, cnt_ref, carry, *, tm, tiles_per_seq):
    i = pl.program_id(0)

    @pl.when(i == 0)
    def _():
        carry[...] = jnp.zeros_like(carry)

    lane = lax.broadcasted_iota(I32, (tm, LANES), 1)
    lo = lane < HEAD_DIM

    lses = [l1_ref[...], l2_ref[...], l3_ref[...]]
    outs = [o1_ref, o2_ref, o3_ref]
    attn_pairs = []
    for hp in range(N_HEADS // 2):
        coef = []
        for par in range(2):
            c0 = (2 * hp + par) * 16
            a = [l[:, c0:c0 + 1] for l in lses]
            mx = jnp.maximum(jnp.maximum(a[0], a[1]), a[2])
            w = [jnp.exp(t - mx) for t in a]
            inv = 1.0 / (w[0] + w[1] + w[2])
            coef.append([t * inv for t in w])
        acc = None
        for g in range(3):
            cg = jnp.where(lo, coef[0][g], coef[1][g])
            term = cg * outs[g][:, hp * LANES:(hp + 1) * LANES].astype(F32)
            acc = term if acc is None else acc + term
        attn_pairs.append(acc)
    attn = jnp.concatenate(attn_pairs, axis=1)

    z = z_ref[...].astype(F32)
    row = lax.broadcasted_iota(I32, (tm, 1), 0)
    seq_first = (i % tiles_per_seq) == 0
    seq_last = (i % tiles_per_seq) == tiles_per_seq - 1
    z_before = jnp.where(seq_first, 0.0, zp_ref[15:16, :].astype(F32))
    z_after = jnp.where(seq_last, 0.0, zn_ref[0:1, :].astype(F32))
    z_up = jnp.where(row == 0, z_before, pltpu.roll(z, 1, 0))
    z_dn = jnp.where(row == tm - 1, z_after, pltpu.roll(z, tm - 1, 0))
    conv = gb_ref[...].astype(F32) * (cw_ref[0:1, :] * z_up + cw_ref[1:2, :] * z + cw_ref[2:3, :] * z_dn)

    na = _rms(attn, ga_ref[...]).astype(BF16)
    nc = _rms(conv, gc_ref[...]).astype(BF16)
    mix = (jnp.dot(na, wo_ref[0:ATTN_W, :], preferred_element_type=F32)
           + jnp.dot(nc, wo_ref[ATTN_W:, :], preferred_element_type=F32))
    x1 = x_ref[...] + mix
    x1_ref[...] = x1
    h2 = _rms(x1, gf_ref[...])
    h2_ref[...] = h2

    logits = jnp.dot(h2, wr_ref[...], preferred_element_type=F32,
                     precision=lax.Precision.HIGHEST) + br_ref[...]
    work = logits
    idx, val = [], []
    for _ in range(TOP_K):
        mk = jnp.max(work, axis=1, keepdims=True)
        ik = jnp.min(jnp.where(work == mk, lane, LANES), axis=1, keepdims=True)
        idx.append(ik)
        val.append(mk)
        work = jnp.where(lane == ik, 2 * NEG, work)
    ex = [jnp.exp(v - val[0]) for v in val]
    inv = 1.0 / (ex[0] + ex[1] + ex[2] + ex[3])
    gates = [e * inv for e in ex]

    hit = [lane == ik for ik in idx]
    onehot = (hit[0] | hit[1] | hit[2] | hit[3]).astype(F32)
    before = jnp.dot(tri_ref[...], onehot.astype(BF16), preferred_element_type=F32) + carry[0:1, :]
    ranks = [jnp.sum(jnp.where(h, before, 0.0), axis=1, keepdims=True) for h in hit]
    new_carry = carry[0:1, :] + jnp.sum(onehot, axis=0, keepdims=True)
    carry[...] = jnp.broadcast_to(new_carry, carry.shape)
    cnt_ref[...] = jnp.broadcast_to(new_carry, cnt_ref.shape)

    ri = jnp.zeros((tm, LANES), I32)
    rg = jnp.zeros((tm, LANES), F32)
    for k in range(TOP_K):
        ri = jnp.where(lane == k, idx[k], ri)
        ri = jnp.where(lane == TOP_K + k, ranks[k].astype(I32), ri)
        rg = jnp.where(lane == k, gates[k], rg)
    ri_ref[...] = ri
    rg_ref[...] = rg


def _mix(x2, o_list, l_list, gb, z, conv_w, ga, gc, w_out_bf, gf, wr_pad, br_pad, tm, seq):
    t, d = x2.shape
    n = t // tm
    tri = jnp.asarray(np.tril(np.ones((tm, tm), np.float32), -1), dtype=BF16)
    rowd = pl.BlockSpec((tm, d), lambda i: (i, 0))
    rowa = pl.BlockSpec((tm, ATTN_W), lambda i: (i, 0))
    rowl = pl.BlockSpec((tm, LANES), lambda i: (i, 0))
    halo = tm // 16
    zprev = pl.BlockSpec((16, CONV_W), lambda i: (jnp.maximum(i * halo - 1, 0), 0))
    znext = pl.BlockSpec((16, CONV_W), lambda i: (jnp.minimum((i + 1) * halo, t // 16 - 1), 0))

    def full(a):
        return pl.BlockSpec(a.shape, lambda i: (0,) * a.ndim)

    return pl.pallas_call(
        functools.partial(_mix_kernel, tm=tm, tiles_per_seq=seq // tm),
        grid=(n,),
        in_specs=[rowd, rowa, rowa, rowa, rowl, rowl, rowl, rowa, rowa, zprev, znext,
                  full(conv_w), full(ga), full(gc), full(w_out_bf), full(gf), full(wr_pad), full(br_pad),
                  full(tri)],
        out_specs=[rowd, rowd, rowl, rowl, pl.BlockSpec((8, LANES), lambda i: (0, 0))],
        out_shape=[jax.ShapeDtypeStruct((t, d), F32), jax.ShapeDtypeStruct((t, d), F32),
                   jax.ShapeDtypeStruct((t, LANES), I32), jax.ShapeDtypeStruct((t, LANES), F32),
                   jax.ShapeDtypeStruct((8, LANES), F32)],
        scratch_shapes=[pltpu.VMEM((8, LANES), F32)],
        compiler_params=_cparams("arbitrary"),
        name="mix_router",
    )(x2, *o_list, *l_list, gb, z, z, z, conv_w, ga, gc, w_out_bf, gf, wr_pad, br_pad, tri)


def _dispatch_kernel(zrow_ref, dest_ref, h_ref, xs_ref, slot_smem, zero_buf, sem_idx, sem_row, sem_zero, *, tm, tb):
    i = pl.program_id(0)
    n = tm * TOP_K

    @pl.when(i == 0)
    def _():
        zero_buf[...] = jnp.zeros_like(zero_buf)
        for j in range(2 * N_EXPERTS):
            @pl.when(zrow_ref[j] >= 0)
            def _():
                start = pl.multiple_of(zrow_ref[j], tb)
                pltpu.make_async_copy(zero_buf, xs_ref.at[pl.ds(start, tb)], sem_zero).start()
        for j in range(2 * N_EXPERTS):
            @pl.when(zrow_ref[j] >= 0)
            def _():
                pltpu.make_async_copy(zero_buf, xs_ref.at[pl.ds(0, tb)], sem_zero).wait()

    cp = pltpu.make_async_copy(dest_ref.at[pl.ds(i * n, n)], slot_smem, sem_idx)
    cp.start()
    cp.wait()

    def body(t, c):
        for k in range(TOP_K):
            slot = slot_smem[t * TOP_K + k]
            pltpu.make_async_copy(h_ref.at[pl.ds(t, 1)], xs_ref.at[pl.ds(slot, 1)], sem_row).start()
        return c

    lax.fori_loop(0, tm, body, 0)
    for _ in range(TOP_K):
        pltpu.make_async_copy(h_ref, xs_ref.at[pl.ds(0, tm)], sem_row).wait()


def _dispatch(zrow, dest_flat, h2, n_slots, tm, tb):
    t, d = h2.shape
    grid_spec = pltpu.PrefetchScalarGridSpec(
        num_scalar_prefetch=1,
        grid=(t // tm,),
        in_specs=[pl.BlockSpec(memory_space=pl.ANY),
                  pl.BlockSpec((tm, d), lambda i, zr: (i, 0))],
        out_specs=pl.BlockSpec(memory_space=pl.ANY),
        scratch_shapes=[pltpu.SMEM((tm * TOP_K,), I32), pltpu.VMEM((tb, d), F32),
                        pltpu.SemaphoreType.DMA, pltpu.SemaphoreType.DMA, pltpu.SemaphoreType.DMA],
    )
    return pl.pallas_call(
        functools.partial(_dispatch_kernel, tm=tm, tb=tb),
        grid_spec=grid_spec,
        out_shape=jax.ShapeDtypeStruct((n_slots, d), F32),
        compiler_params=_cparams("arbitrary"),
        name="dispatch",
    )(zrow, dest_flat, h2)


def _expert_kernel(be_ref, bx_ref, bv_ref, xs_ref, wgu_ref, bgu_ref, wd_ref, bd_ref, o_ref, *, dff):
    i = pl.program_id(0)

    @pl.when(bv_ref[i] > 0)
    def _():
        x = xs_ref[...].astype(BF16)
        hu = jnp.dot(x, wgu_ref[0], preferred_element_type=F32) + bgu_ref[0]
        a = jnp.minimum(hu[:, :dff], SWIGLU_LIMIT)
        lin = jnp.clip(hu[:, dff:], -SWIGLU_LIMIT, SWIGLU_LIMIT)
        act = a * (1.0 / (1.0 + jnp.exp(-SWIGLU_ALPHA * a))) * (lin + 1.0)
        o_ref[...] = jnp.dot(act.astype(BF16), wd_ref[0], preferred_element_type=F32) + bd_ref[0]

    @pl.when(bv_ref[i] == 0)
    def _():
        o_ref[...] = jnp.zeros_like(o_ref)


def _experts(blk_e, blk_x, blk_v, xs, wgu_bf, bgu, wd_bf, bd, tb):
    n_slots, d = xs.shape
    dff = wd_bf.shape[1]
    nb = n_slots // tb
    grid_spec = pltpu.PrefetchScalarGridSpec(
        num_scalar_prefetch=3,
        grid=(nb,),
        in_specs=[pl.BlockSpec((tb, d), lambda i, be, bx, bv: (bx[i], 0)),
                  pl.BlockSpec((1, d, 2 * dff), lambda i, be, bx, bv: (be[i], 0, 0)),
                  pl.BlockSpec((1, 1, 2 * dff), lambda i, be, bx, bv: (be[i], 0, 0)),
                  pl.BlockSpec((1, dff, d), lambda i, be, bx, bv: (be[i], 0, 0)),
                  pl.BlockSpec((1, 1, d), lambda i, be, bx, bv: (be[i], 0, 0))],
        out_specs=pl.BlockSpec((tb, d), lambda i, be, bx, bv: (i, 0)),
    )
    return pl.pallas_call(
        functools.partial(_expert_kernel, dff=dff),
        grid_spec=grid_spec,
        out_shape=jax.ShapeDtypeStruct((n_slots, d), F32),
        compiler_params=_cparams("arbitrary"),
        name="experts",
    )(blk_e, blk_x, blk_v, xs, wgu_bf, bgu, wd_bf, bd)


def _combine_kernel(dest_ref, x1_ref, rg_ref, fg_ref, mo_ref, out_ref, slot_smem, gbuf, sem_idx, sem_row, *, tm):
    i = pl.program_id(0)
    n = tm * TOP_K
    cp = pltpu.make_async_copy(dest_ref.at[pl.ds(i * n, n)], slot_smem, sem_idx)
    cp.start()
    cp.wait()

    def body(t, c):
        for k in range(TOP_K):
            slot = slot_smem[t * TOP_K + k]
            pltpu.make_async_copy(mo_ref.at[pl.ds(slot, 1)], gbuf.at[k, pl.ds(t, 1)], sem_row).start()
        return c

    lax.fori_loop(0, tm, body, 0)
    for k in range(TOP_K):
        pltpu.make_async_copy(mo_ref.at[pl.ds(0, tm)], gbuf.at[k], sem_row).wait()

    y = x1_ref[...]
    g = rg_ref[...]
    for k in range(TOP_K):
        y = y + g[:, k:k + 1] * gbuf[k]
    out_ref[...] = _rms(y, fg_ref[...])


def _combine(dest_flat, x1, rg, fg, mlp_out, tm):
    t, d = x1.shape
    rowd = pl.BlockSpec((tm, d), lambda i: (i, 0))
    return pl.pallas_call(
        functools.partial(_combine_kernel, tm=tm),
        grid=(t // tm,),
        in_specs=[pl.BlockSpec(memory_space=pl.ANY), rowd,
                  pl.BlockSpec((tm, LANES), lambda i: (i, 0)),
                  pl.BlockSpec((1, d), lambda i: (0, 0)),
                  pl.BlockSpec(memory_space=pl.ANY)],
        out_specs=rowd,
        out_shape=jax.ShapeDtypeStruct((t, d), F32),
        scratch_shapes=[pltpu.SMEM((tm * TOP_K,), I32), pltpu.VMEM((TOP_K, tm, d), F32),
                        pltpu.SemaphoreType.DMA, pltpu.SemaphoreType.DMA],
        compiler_params=_cparams("arbitrary"),
        name="combine",
    )(dest_flat, x1, rg, fg, mlp_out)


def _block_tables(counts, idx, rank, tb, nb):
    padded = (counts + tb - 1) // tb * tb
    pad_end = jnp.cumsum(padded)
    pad_start = pad_end - padded
    sel = idx[..., None] == jnp.arange(N_EXPERTS, dtype=I32)
    dest = jnp.sum(jnp.where(sel, pad_start, 0), axis=-1) + rank
    blocks_e = padded // tb
    blk_end = jnp.cumsum(blocks_e)
    total = blk_end[-1]
    bid = jnp.arange(nb, dtype=I32)
    src = jnp.minimum(bid, total - 1)
    e = jnp.minimum(jnp.sum((src[:, None] >= blk_end[None, :]).astype(I32), axis=1), N_EXPERTS - 1)
    first = jnp.sum(jnp.where(e[:, None] == jnp.arange(N_EXPERTS), blk_end - blocks_e, 0), axis=1)
    cnt = jnp.sum(jnp.where(e[:, None] == jnp.arange(N_EXPERTS), counts, 0), axis=1)
    valid = jnp.where(bid < total, jnp.clip(cnt - (src - first) * tb, 0, tb), 0)
    tail = total + jnp.arange(N_EXPERTS, dtype=I32)
    zrow = jnp.concatenate([jnp.where(padded > 0, pad_end - tb, -1), jnp.where(tail < nb, tail * tb, -1)])
    return dest.astype(I32), e.astype(I32), src.astype(I32), valid.astype(I32), zrow.astype(I32)


def kernel(x, mix_norm_g, w_in, conv_w, attn_out_norm_g, conv_out_norm_g, w_out, ffn_norm_g, w_router,
           b_router, w_gate_up, b_gate_up, w_down, b_down, final_norm_g):
    batch, seq, d = x.shape
    t = batch * seq
    assert w_in.shape[0] == 1, "single-layer trunk: the final RMSNorm is fused into the combine kernel"
    tm = 512
    tb = 512
    nb = (t * TOP_K) // tb + N_EXPERTS
    x2 = x.reshape(t, d)

    q, k, v, gb, z = _proj(x2, mix_norm_g[0][None, :], w_in[0].astype(BF16), tm)
    o_list, l_list = [], []
    for _, dil in BRANCHES:
        o, lse = _attn_branch(q, k, v, dil, batch, seq)
        o_list.append(o)
        l_list.append(lse)

    wr_pad = jnp.zeros((d, LANES), F32).at[:, :N_EXPERTS].set(w_router[0])
    br_pad = jnp.full((1, LANES), NEG, F32).at[0, :N_EXPERTS].set(b_router[0])
    x1, h2, ri, rg, cnt = _mix(x2, o_list, l_list, gb, z, conv_w[0], attn_out_norm_g[0][None, :],
                               conv_out_norm_g[0][None, :], w_out[0].astype(BF16), ffn_norm_g[0][None, :],
                               wr_pad, br_pad, tm, seq)

    counts = cnt[0, :N_EXPERTS].astype(I32)
    dest, blk_e, blk_x, blk_v, zrow = _block_tables(counts, ri[:, :TOP_K], ri[:, TOP_K:2 * TOP_K], tb, nb)
    dest_flat = dest.reshape(t * TOP_K)

    xs = _dispatch(zrow, dest_flat, h2, nb * tb, tm, tb)
    mo = _experts(blk_e, blk_x, blk_v, xs, w_gate_up[0].astype(BF16), b_gate_up[0][:, None, :],
                  w_down[0].astype(BF16), b_down[0][:, None, :], tb)
    out = _combine(dest_flat, x1, rg, final_norm_g[None, :], mo, 256)
    return out.reshape(batch, seq, d)
```

```python
import functools

import numpy as np
import jax
import jax.numpy as jnp
from jax import lax
from jax.experimental import pallas as pl
from jax.experimental.pallas import tpu as pltpu

F32 = jnp.float32
BF16 = jnp.bfloat16
I32 = jnp.int32

HEAD_DIM = 64
N_HEADS = 8
ATTN_W = N_HEADS * HEAD_DIM
CONV_W = 512
N_EXPERTS = 32
TOP_K = 4
DILATIONS = (1, 4, 16)
SIDE = 64
SWIGLU_ALPHA = 1.702
SWIGLU_LIMIT = 7.0
RMS_EPS = 1e-5
NEG = -1e30

LANES = 128
SLABS = ATTN_W // LANES
Q_SUB = 128
K_WIN = Q_SUB + 2 * SIDE
VMEM_LIMIT = 56 * 1024 * 1024


def _cparams(*sem):
    return pltpu.CompilerParams(dimension_semantics=sem, vmem_limit_bytes=VMEM_LIMIT)


def _rms(x, g):
    return x * lax.rsqrt(jnp.mean(x * x, axis=-1, keepdims=True) + RMS_EPS) * g


def _split_bf16(a):
    hi = a.astype(BF16)
    return hi, (a - hi.astype(F32)).astype(BF16)


def _proj_kernel(x_ref, g_ref, w_ref, *refs, tm):
    outs = refs[:9]
    gb_ref, z_ref, pbuf = refs[9:]
    h = _rms(x_ref[...], g_ref[...]).astype(BF16)

    def proj(j):
        return jnp.dot(h, w_ref[:, j * ATTN_W:(j + 1) * ATTN_W], preferred_element_type=F32)

    for j in range(3):
        p = proj(j)
        if j == 0:
            p = p * (HEAD_DIM ** -0.5)
        outs[3 * j][0] = p.astype(BF16)
        for c in range(SLABS):
            pbuf[j, c] = p[:, c * LANES:(c + 1) * LANES]
        for di in (1, 2):
            dil = DILATIONS[di]
            o = outs[3 * j + di]
            for r in range(dil):
                for c in range(SLABS):
                    lo = r * ATTN_W + c * LANES
                    o[0, :, lo:lo + LANES] = pbuf[j, c, pl.ds(r, tm // dil, stride=dil), :].astype(BF16)
    gb_ref[...] = proj(3).astype(BF16)
    z_ref[...] = (proj(4) * proj(5)).astype(BF16)


def _proj(x2, g, w_in_bf, tm, batch, seq):
    t, d = x2.shape
    tps = seq // tm
    row = pl.BlockSpec((tm, ATTN_W), lambda i: (i, 0))
    view_specs, view_shapes = [], []
    for _ in range(3):
        for dil in DILATIONS:
            view_specs.append(pl.BlockSpec((1, tm // dil, dil * ATTN_W), lambda i: (i // tps, i % tps, 0)))
            view_shapes.append(jax.ShapeDtypeStruct((batch, seq // dil, dil * ATTN_W), BF16))
    flat = jax.ShapeDtypeStruct((t, ATTN_W), BF16)
    return pl.pallas_call(
        functools.partial(_proj_kernel, tm=tm),
        grid=(t // tm,),
        in_specs=[pl.BlockSpec((tm, d), lambda i: (i, 0)),
                  pl.BlockSpec((1, d), lambda i: (0, 0)),
                  pl.BlockSpec(w_in_bf.shape, lambda i: (0, 0))],
        out_specs=view_specs + [row, row],
        out_shape=view_shapes + [flat, flat],
        scratch_shapes=[pltpu.VMEM((3, SLABS, tm, LANES), F32)],
        compiler_params=_cparams("parallel"),
        name="proj",
    )(x2, g, w_in_bf)


def _bias_tables(dil):
    row = np.arange(Q_SUB)[:, None]
    col = np.arange(K_WIN)[None, :]
    rel = col - SIDE - row
    band = np.abs(rel) <= SIDE
    slopes = 2.0 ** (-8.0 * np.arange(1, N_HEADS + 1) / N_HEADS)
    tabs = []
    for t in range(4):
        ok = band.copy()
        if t & 1:
            ok &= col >= SIDE
        if t & 2:
            ok &= col < SIDE + Q_SUB
        for h in range(N_HEADS):
            tabs.append(np.where(ok, -slopes[h] * np.abs(rel) * dil, NEG))
    return jnp.asarray(np.stack(tabs), dtype=F32)


def _attn_kernel(q_ref, kp_ref, km_ref, kn_ref, vp_ref, vm_ref, vn_ref, bias_ref,
                 o_ref, lse_ref, kbuf, vbuf, *, tq, n_sub_total):
    i = pl.program_id(2)
    kbuf[0:SIDE] = kp_ref[0]
    kbuf[SIDE:SIDE + tq] = km_ref[0]
    kbuf[SIDE + tq:] = kn_ref[0]
    vbuf[0:SIDE] = vp_ref[0]
    vbuf[SIDE:SIDE + tq] = vm_ref[0]
    vbuf[SIDE + tq:] = vn_ref[0]

    lane = lax.broadcasted_iota(I32, (Q_SUB, LANES), 1)
    lo = lane < HEAD_DIM
    grp = lane // 16
    n_sub = tq // Q_SUB

    def sub(j, carry):
        r0 = pl.multiple_of(j * Q_SUB, Q_SUB)
        g_sub = i * n_sub + j
        tbl = (g_sub == 0).astype(I32) + 2 * (g_sub == n_sub_total - 1).astype(I32)
        lse_tile = jnp.zeros((Q_SUB, LANES), F32)
        for hp in range(N_HEADS // 2):
            cols = slice(hp * LANES, (hp + 1) * LANES)
            qp = q_ref[0, pl.ds(r0, Q_SUB), cols]
            kw = kbuf[pl.ds(r0, K_WIN), cols]
            vw = vbuf[pl.ds(r0, K_WIN), cols]
            outs = []
            for par in range(2):
                qm = jnp.where(lo if par == 0 else jnp.logical_not(lo), qp, jnp.zeros_like(qp))
                s = lax.dot_general(qm, kw, (((1,), (1,)), ((), ())), preferred_element_type=F32)
                s = s + bias_ref[tbl * N_HEADS + 2 * hp + par]
                m = jnp.max(s, axis=1, keepdims=True)
                p = jnp.exp(s - m)
                l = jnp.sum(p, axis=1, keepdims=True)
                pv = jnp.dot(p.astype(BF16), vw, preferred_element_type=F32)
                outs.append(pv / l)
                lse_h = m + jnp.log(l)
                lse_tile = jnp.where(grp == 2 * hp + par, lse_h, lse_tile)
            o_ref[0, pl.ds(r0, Q_SUB), cols] = jnp.where(lo, outs[0], outs[1]).astype(BF16)
        lse_ref[0, pl.ds(r0, Q_SUB), :] = lse_tile
        return carry

    lax.fori_loop(0, n_sub, sub, 0)


def _attn_branch(qv, kv, vv, dil):
    batch, sd, _ = qv.shape
    tq = min(512, sd)
    n_tiles = sd // tq
    bias = _bias_tables(dil)
    r_side = tq // SIDE
    last_side = sd // SIDE - 1

    main = pl.BlockSpec((1, tq, ATTN_W), lambda b, r, i: (b, i, r))
    prev = pl.BlockSpec((1, SIDE, ATTN_W), lambda b, r, i: (b, jnp.maximum(i * r_side - 1, 0), r))
    nxt = pl.BlockSpec((1, SIDE, ATTN_W), lambda b, r, i: (b, jnp.minimum((i + 1) * r_side, last_side), r))
    lse_spec = pl.BlockSpec((1, tq, LANES), lambda b, r, i: (b, i, r))
    return pl.pallas_call(
        functools.partial(_attn_kernel, tq=tq, n_sub_total=sd // Q_SUB),
        grid=(batch, dil, n_tiles),
        in_specs=[main, prev, main, nxt, prev, main, nxt,
                  pl.BlockSpec(bias.shape, lambda b, r, i: (0, 0, 0))],
        out_specs=[main, lse_spec],
        out_shape=[jax.ShapeDtypeStruct((batch, sd, dil * ATTN_W), BF16),
                   jax.ShapeDtypeStruct((batch, sd, dil * LANES), F32)],
        scratch_shapes=[pltpu.VMEM((tq + 2 * SIDE, ATTN_W), BF16),
                        pltpu.VMEM((tq + 2 * SIDE, ATTN_W), BF16)],
        compiler_params=_cparams("parallel", "parallel", "parallel"),
        name=f"attn_d{dil}",
    )(qv, kv, kv, kv, vv, vv, vv, bias)


def _mix_kernel(x_ref, o1_ref, o4_ref, o16_ref, l1_ref, l4_ref, l16_ref, gb_ref, z_ref, zp_ref, zn_ref,
                cw_ref, ga_ref, gc_ref, wo_ref, gf_ref, wrh_ref, wrl_ref, br_ref, upper_ref, expand_ref,
                x1_ref, h2_ref, ri_ref, rg_ref, cnt_ref, onat, lnat, carry, *, tm, tiles_per_seq):
    i = pl.program_id(0)

    @pl.when(i == 0)
    def _():
        carry[...] = jnp.zeros_like(carry)

    for bi, (o_ref, l_ref, dil) in enumerate(((o4_ref, l4_ref, 4), (o16_ref, l16_ref, 16))):
        n = tm // dil
        for r in range(dil):
            lnat[bi, pl.ds(r, n, stride=dil), :] = l_ref[0, :, r * LANES:(r + 1) * LANES]
            for c in range(SLABS):
                lo = r * ATTN_W + c * LANES
                onat[bi, c, pl.ds(r, n, stride=dil), :] = o_ref[0, :, lo:lo + LANES].astype(F32)

    lse = [l1_ref[0], lnat[0], lnat[1]]
    mx = jnp.maximum(jnp.maximum(lse[0], lse[1]), lse[2])
    w = [jnp.exp(t - mx) for t in lse]
    inv = 1.0 / (w[0] + w[1] + w[2])
    wide = []
    for g in range(3):
        hi, lo_part = _split_bf16(w[g] * inv)
        wide.append(jnp.dot(hi, expand_ref[...], preferred_element_type=F32)
                    + jnp.dot(lo_part, expand_ref[...], preferred_element_type=F32))
    slabs = []
    for c in range(SLABS):
        cs = slice(c * LANES, (c + 1) * LANES)
        slabs.append(wide[0][:, cs] * o1_ref[0, :, cs].astype(F32)
                     + wide[1][:, cs] * onat[0, c] + wide[2][:, cs] * onat[1, c])
    attn = jnp.concatenate(slabs, axis=1)

    z = z_ref[...].astype(F32)
    row = lax.broadcasted_iota(I32, (tm, 1), 0)
    seq_first = (i % tiles_per_seq) == 0
    seq_last = (i % tiles_per_seq) == tiles_per_seq - 1
    z_before = jnp.where(seq_first, 0.0, zp_ref[15:16, :].astype(F32))
    z_after = jnp.where(seq_last, 0.0, zn_ref[0:1, :].astype(F32))
    z_up = jnp.where(row == 0, z_before, pltpu.roll(z, 1, 0))
    z_dn = jnp.where(row == tm - 1, z_after, pltpu.roll(z, tm - 1, 0))
    conv = gb_ref[...].astype(F32) * (cw_ref[0:1, :] * z_up + cw_ref[1:2, :] * z + cw_ref[2:3, :] * z_dn)

    na = _rms(attn, ga_ref[...]).astype(BF16)
    nc = _rms(conv, gc_ref[...]).astype(BF16)
    mix = (jnp.dot(na, wo_ref[0:ATTN_W, :], preferred_element_type=F32)
           + jnp.dot(nc, wo_ref[ATTN_W:, :], preferred_element_type=F32))
    x1 = x_ref[...] + mix
    x1_ref[...] = x1
    h2 = _rms(x1, gf_ref[...])
    h2_ref[...] = h2

    hh, hl = _split_bf16(h2)
    logits = (jnp.dot(hh, wrh_ref[...], preferred_element_type=F32)
              + jnp.dot(hl, wrh_ref[...], preferred_element_type=F32)
              + jnp.dot(hh, wrl_ref[...], preferred_element_type=F32) + br_ref[...])
    lt = logits.T[0:N_EXPERTS, :]

    erow = lax.broadcasted_iota(I32, (N_EXPERTS, tm), 0)
    work = lt
    idx, val = [], []
    for _ in range(TOP_K):
        mk = jnp.max(work, axis=0, keepdims=True)
        ik = jnp.min(jnp.where(work == mk, erow, N_EXPERTS), axis=0, keepdims=True)
        idx.append(ik)
        val.append(mk)
        work = jnp.where(erow == ik, 2 * NEG, work)
    ex = [jnp.exp(v - val[0]) for v in val]
    ginv = 1.0 / (ex[0] + ex[1] + ex[2] + ex[3])
    gates = [e * ginv for e in ex]

    hit = [erow == ik for ik in idx]
    onehot = (hit[0] | hit[1] | hit[2] | hit[3]).astype(F32)
    before = jnp.dot(onehot.astype(BF16), upper_ref[...], preferred_element_type=F32) + carry[:, 0:1]
    ranks = [jnp.sum(jnp.where(h, before, 0.0), axis=0, keepdims=True) for h in hit]
    new_carry = carry[:, 0:1] + jnp.sum(onehot, axis=1, keepdims=True)
    carry[...] = jnp.broadcast_to(new_carry, carry.shape)
    cnt_ref[...] = jnp.broadcast_to(new_carry, cnt_ref.shape)

    ri_ref[...] = jnp.concatenate(idx + [r.astype(I32) for r in ranks], axis=0)
    rg_ref[...] = jnp.concatenate(gates + [jnp.zeros((TOP_K, tm), F32)], axis=0)


def _mix(x2, o_views, l_views, gb, z, conv_w, ga, gc, w_out_bf, gf, w_router, b_router, tm, batch, seq):
    t, d = x2.shape
    n = t // tm
    tps = seq // tm
    upper = jnp.asarray(np.triu(np.ones((tm, tm), np.float32), 1), dtype=BF16)
    expand = jnp.asarray(np.arange(LANES)[:, None] == 16 * (np.arange(ATTN_W)[None, :] // HEAD_DIM), dtype=BF16)
    wr_pad = jnp.zeros((d, LANES), F32).at[:, :N_EXPERTS].set(w_router)
    wr_hi, wr_lo = _split_bf16(wr_pad)
    br_pad = jnp.zeros((1, LANES), F32).at[0, :N_EXPERTS].set(b_router)

    rowd = pl.BlockSpec((tm, d), lambda i: (i, 0))
    rowa = pl.BlockSpec((tm, ATTN_W), lambda i: (i, 0))
    halo = tm // 16
    zprev = pl.BlockSpec((16, CONV_W), lambda i: (jnp.maximum(i * halo - 1, 0), 0))
    znext = pl.BlockSpec((16, CONV_W), lambda i: (jnp.minimum((i + 1) * halo, t // 16 - 1), 0))

    def view(width, dil):
        return pl.BlockSpec((1, tm // dil, dil * width), lambda i: (i // tps, i % tps, 0))

    def full(a):
        return pl.BlockSpec(a.shape, lambda i: (0,) * a.ndim)

    consts = (conv_w, ga, gc, w_out_bf, gf, wr_hi, wr_lo, br_pad, upper, expand)
    return pl.pallas_call(
        functools.partial(_mix_kernel, tm=tm, tiles_per_seq=tps),
        grid=(n,),
        in_specs=[rowd] + [view(ATTN_W, dil) for dil in DILATIONS] + [view(LANES, dil) for dil in DILATIONS]
                 + [rowa, rowa, zprev, znext] + [full(a) for a in consts],
        out_specs=[rowd, rowd, pl.BlockSpec((8, tm), lambda i: (0, i)), pl.BlockSpec((8, tm), lambda i: (0, i)),
                   pl.BlockSpec((N_EXPERTS, LANES), lambda i: (0, 0))],
        out_shape=[jax.ShapeDtypeStruct((t, d), F32), jax.ShapeDtypeStruct((t, d), F32),
                   jax.ShapeDtypeStruct((8, t), I32), jax.ShapeDtypeStruct((8, t), F32),
                   jax.ShapeDtypeStruct((N_EXPERTS, LANES), F32)],
        scratch_shapes=[pltpu.VMEM((2, SLABS, tm, LANES), F32), pltpu.VMEM((2, tm, LANES), F32),
                        pltpu.VMEM((N_EXPERTS, LANES), F32)],
        compiler_params=_cparams("arbitrary"),
        name="mix_router",
    )(x2, *o_views, *l_views, gb, z, z, z, *consts)


def _dispatch_kernel(zrow_ref, dest_ref, h_ref, xs_ref, slot_smem, zero_buf, sem_idx, sem_row, sem_zero,
                     *, tm, tb, n_tiles):
    i = pl.program_id(0)
    n = tm * TOP_K

    def idx_copy(tile, buf):
        return pltpu.make_async_copy(dest_ref.at[pl.ds(tile * n, n)], slot_smem.at[buf], sem_idx.at[buf])

    def drain_tile(par):
        for _ in range(TOP_K):
            pltpu.make_async_copy(h_ref.at[pl.ds(0, tm)], xs_ref.at[pl.ds(0, tm)], sem_row.at[par]).wait()

    @pl.when(i == 0)
    def _():
        zero_buf[...] = jnp.zeros_like(zero_buf)
        for j in range(2 * N_EXPERTS):
            @pl.when(zrow_ref[j] >= 0)
            def _():
                start = pl.multiple_of(zrow_ref[j], tb)
                pltpu.make_async_copy(zero_buf, xs_ref.at[pl.ds(start, tb)], sem_zero).start()
        for j in range(2 * N_EXPERTS):
            @pl.when(zrow_ref[j] >= 0)
            def _():
                pltpu.make_async_copy(zero_buf, xs_ref.at[pl.ds(0, tb)], sem_zero).wait()
        idx_copy(0, 0).start()

    cur = i % 2
    idx_copy(i, cur).wait()

    @pl.when(i + 1 < n_tiles)
    def _():
        idx_copy(i + 1, 1 - cur).start()

    base = i * tm

    def body(t, c):
        for k in range(TOP_K):
            slot = slot_smem[cur, k * tm + t]
            pltpu.make_async_copy(h_ref.at[pl.ds(base + t, 1)], xs_ref.at[pl.ds(slot, 1)],
                                  sem_row.at[cur]).start(priority=k % 2)
        return c

    lax.fori_loop(0, tm, body, 0, unroll=8)

    @pl.when(i > 0)
    def _():
        drain_tile(1 - cur)

    @pl.when(i == n_tiles - 1)
    def _():
        drain_tile(cur)


def _dispatch(zrow, dest_tiles, h2, n_slots, tm, tb):
    t, d = h2.shape
    n_tiles = t // tm
    grid_spec = pltpu.PrefetchScalarGridSpec(
        num_scalar_prefetch=1,
        grid=(n_tiles,),
        in_specs=[pl.BlockSpec(memory_space=pl.ANY), pl.BlockSpec(memory_space=pl.ANY)],
        out_specs=pl.BlockSpec(memory_space=pl.ANY),
        scratch_shapes=[pltpu.SMEM((2, tm * TOP_K), I32), pltpu.VMEM((tb, d), F32),
                        pltpu.SemaphoreType.DMA((2,)), pltpu.SemaphoreType.DMA((2,)), pltpu.SemaphoreType.DMA],
    )
    return pl.pallas_call(
        functools.partial(_dispatch_kernel, tm=tm, tb=tb, n_tiles=n_tiles),
        grid_spec=grid_spec,
        out_shape=jax.ShapeDtypeStruct((n_slots, d), F32),
        compiler_params=_cparams("arbitrary"),
        name="dispatch",
    )(zrow, dest_tiles, h2)


def _expert_kernel(be_ref, bx_ref, bv_ref, xs_ref, wgu_ref, bgu_ref, wd_ref, bd_ref, o_ref, wgu_bf, wd_bf, *, dff):
    i = pl.program_id(0)

    @pl.when(jnp.logical_or(i == 0, be_ref[i] != be_ref[jnp.maximum(i - 1, 0)]))
    def _():
        wgu_bf[...] = wgu_ref[0].astype(BF16)
        wd_bf[...] = wd_ref[0].astype(BF16)

    @pl.when(bv_ref[i] > 0)
    def _():
        x = xs_ref[...].astype(BF16)
        hu = jnp.dot(x, wgu_bf[...], preferred_element_type=F32) + bgu_ref[0]
        a = jnp.minimum(hu[:, :dff], SWIGLU_LIMIT)
        lin = jnp.clip(hu[:, dff:], -SWIGLU_LIMIT, SWIGLU_LIMIT)
        act = a * (1.0 / (1.0 + jnp.exp(-SWIGLU_ALPHA * a))) * (lin + 1.0)
        o_ref[...] = jnp.dot(act.astype(BF16), wd_bf[...], preferred_element_type=F32) + bd_ref[0]

    @pl.when(bv_ref[i] == 0)
    def _():
        o_ref[...] = jnp.zeros_like(o_ref)


def _experts(blk_e, blk_x, blk_v, xs, wgu, bgu, wd, bd, tb):
    n_slots, d = xs.shape
    dff = wd.shape[1]
    nb = n_slots // tb
    grid_spec = pltpu.PrefetchScalarGridSpec(
        num_scalar_prefetch=3,
        grid=(nb,),
        in_specs=[pl.BlockSpec((tb, d), lambda i, be, bx, bv: (bx[i], 0)),
                  pl.BlockSpec((1, d, 2 * dff), lambda i, be, bx, bv: (be[i], 0, 0)),
                  pl.BlockSpec((1, 1, 2 * dff), lambda i, be, bx, bv: (be[i], 0, 0)),
                  pl.BlockSpec((1, dff, d), lambda i, be, bx, bv: (be[i], 0, 0)),
                  pl.BlockSpec((1, 1, d), lambda i, be, bx, bv: (be[i], 0, 0))],
        out_specs=pl.BlockSpec((tb, d), lambda i, be, bx, bv: (i, 0)),
        scratch_shapes=[pltpu.VMEM((d, 2 * dff), BF16), pltpu.VMEM((dff, d), BF16)],
    )
    return pl.pallas_call(
        functools.partial(_expert_kernel, dff=dff),
        grid_spec=grid_spec,
        out_shape=jax.ShapeDtypeStruct((n_slots, d), F32),
        compiler_params=_cparams("arbitrary"),
        name="experts",
    )(blk_e, blk_x, blk_v, xs, wgu, bgu, wd, bd)


def _combine_kernel(dest_ref, x1_ref, rg_ref, fg_ref, mo_ref, out_ref, slot_smem, gbuf, sem_idx, sem_row,
                    *, tm, n_tiles):
    i = pl.program_id(0)
    n = tm * TOP_K

    def idx_copy(tile):
        buf = tile % 3
        return pltpu.make_async_copy(dest_ref.at[pl.ds(tile * n, n)], slot_smem.at[buf], sem_idx.at[buf])

    def issue(tile):
        sb, gb = tile % 3, tile % 2

        def body(t, c):
            for k in range(TOP_K):
                slot = slot_smem[sb, k * tm + t]
                pltpu.make_async_copy(mo_ref.at[pl.ds(slot, 1)], gbuf.at[gb, k, pl.ds(t, 1)],
                                      sem_row.at[gb]).start(priority=k % 2)
            return c

        lax.fori_loop(0, tm, body, 0, unroll=8)

    @pl.when(i == 0)
    def _():
        idx_copy(0).start()
        if n_tiles > 1:
            idx_copy(1).start()
        idx_copy(0).wait()
        issue(0)

    @pl.when(i + 1 < n_tiles)
    def _():
        idx_copy(i + 1).wait()
        issue(i + 1)

    @pl.when(i + 2 < n_tiles)
    def _():
        idx_copy(i + 2).start()

    cur = i % 2
    for k in range(TOP_K):
        pltpu.make_async_copy(mo_ref.at[pl.ds(0, tm)], gbuf.at[cur, k], sem_row.at[cur]).wait()

    g_rows = jnp.concatenate([rg_ref[...], jnp.zeros((LANES - 8, tm), F32)], axis=0)
    g = g_rows.T
    y = x1_ref[...]
    for k in range(TOP_K):
        y = y + g[:, k:k + 1] * gbuf[cur, k]
    out_ref[...] = _rms(y, fg_ref[...])


def _combine(dest_tiles, x1, rg, fg, mlp_out, tm):
    t, d = x1.shape
    n_tiles = t // tm
    rowd = pl.BlockSpec((tm, d), lambda i: (i, 0))
    return pl.pallas_call(
        functools.partial(_combine_kernel, tm=tm, n_tiles=n_tiles),
        grid=(n_tiles,),
        in_specs=[pl.BlockSpec(memory_space=pl.ANY), rowd,
                  pl.BlockSpec((8, tm), lambda i: (0, i)),
                  pl.BlockSpec((1, d), lambda i: (0, 0)),
                  pl.BlockSpec(memory_space=pl.ANY)],
        out_specs=rowd,
        out_shape=jax.ShapeDtypeStruct((t, d), F32),
        scratch_shapes=[pltpu.SMEM((3, tm * TOP_K), I32), pltpu.VMEM((2, TOP_K, tm, d), F32),
                        pltpu.SemaphoreType.DMA((3,)), pltpu.SemaphoreType.DMA((2,))],
        compiler_params=_cparams("arbitrary"),
        name="combine",
    )(dest_tiles, x1, rg, fg, mlp_out)


def _block_tables(counts, idx, rank, tb, nb):
    padded = (counts + tb - 1) // tb * tb
    pad_end = jnp.cumsum(padded)
    pad_start = pad_end - padded
    sel = idx[..., None] == jnp.arange(N_EXPERTS, dtype=I32)
    dest = jnp.sum(jnp.where(sel, pad_start, 0), axis=-1) + rank
    blocks_e = padded // tb
    blk_end = jnp.cumsum(blocks_e)
    total = blk_end[-1]
    bid = jnp.arange(nb, dtype=I32)
    src = jnp.minimum(bid, total - 1)
    e = jnp.minimum(jnp.sum((src[:, None] >= blk_end[None, :]).astype(I32), axis=1), N_EXPERTS - 1)
    first = jnp.sum(jnp.where(e[:, None] == jnp.arange(N_EXPERTS), blk_end - blocks_e, 0), axis=1)
    cnt = jnp.sum(jnp.where(e[:, None] == jnp.arange(N_EXPERTS), counts, 0), axis=1)
    valid = jnp.where(bid < total, jnp.clip(cnt - (src - first) * tb, 0, tb), 0)
    tail = total + jnp.arange(N_EXPERTS, dtype=I32)
    zrow = jnp.concatenate([jnp.where(padded > 0, pad_end - tb, -1), jnp.where(tail < nb, tail * tb, -1)])
    return dest.astype(I32), e.astype(I32), src.astype(I32), valid.astype(I32), zrow.astype(I32)


def _tile_major(dest, tm):
    k, t = dest.shape
    return dest.reshape(k, t // tm, tm).transpose(1, 0, 2).reshape(-1)


def kernel(x, mix_norm_g, w_in, conv_w, attn_out_norm_g, conv_out_norm_g, w_out, ffn_norm_g, w_router,
           b_router, w_gate_up, b_gate_up, w_down, b_down, final_norm_g):
    batch, seq, d = x.shape
    t = batch * seq
    assert w_in.shape[0] == 1, "single-layer trunk: the final RMSNorm is fused into the combine kernel"
    tm = 512
    tb = 512
    tc = 256
    nb = (t * TOP_K) // tb + N_EXPERTS
    x2 = x.reshape(t, d)

    *qkv, gb, z = _proj(x2, mix_norm_g[0][None, :], w_in[0].astype(BF16), tm, batch, seq)
    o_views, l_views = [], []
    for di, dil in enumerate(DILATIONS):
        o, lse = _attn_branch(qkv[di], qkv[3 + di], qkv[6 + di], dil)
        o_views.append(o)
        l_views.append(lse)

    x1, h2, ri, rg, cnt = _mix(x2, o_views, l_views, gb, z, conv_w[0], attn_out_norm_g[0][None, :],
                               conv_out_norm_g[0][None, :], w_out[0].astype(BF16), ffn_norm_g[0][None, :],
                               w_router[0], b_router[0], tm, batch, seq)

    counts = cnt[:, 0].astype(I32)
    dest, blk_e, blk_x, blk_v, zrow = _block_tables(counts, ri[:TOP_K], ri[TOP_K:], tb, nb)

    xs = _dispatch(zrow, _tile_major(dest, tm), h2, nb * tb, tm, tb)
    mo = _experts(blk_e, blk_x, blk_v, xs, w_gate_up[0], b_gate_up[0][:, None, :],
                  w_down[0], b_down[0][:, None, :], tb)
    out = _combine(_tile_major(dest, tc), x1, rg, final_norm_g[None, :], mo, tc)
    return out.reshape(batch, seq, d)
```

```python
import functools

import numpy as np
import jax
import jax.numpy as jnp
from jax import lax
from jax.experimental import pallas as pl
from jax.experimental.pallas import tpu as pltpu

F32 = jnp.float32
BF16 = jnp.bfloat16
I32 = jnp.int32

HEAD_DIM = 64
N_HEADS = 8
ATTN_W = N_HEADS * HEAD_DIM
CONV_W = 512
N_EXPERTS = 32
TOP_K = 4
DILATIONS = (1, 4, 16)
SIDE = 64
SWIGLU_ALPHA = 1.702
SWIGLU_LIMIT = 7.0
RMS_EPS = 1e-5
NEG = -1e30
LOG2E = 1.4426950408889634

LANES = 128
SLABS = ATTN_W // LANES
Q_SUB = 128
K_WIN = Q_SUB + 2 * SIDE
VMEM_LIMIT = 56 * 1024 * 1024


def _cparams(*sem):
    return pltpu.CompilerParams(dimension_semantics=sem, vmem_limit_bytes=VMEM_LIMIT)


def _rms(x, g):
    return x * lax.rsqrt(jnp.mean(x * x, axis=-1, keepdims=True) + RMS_EPS) * g


SUBLANES = 8


def _store_token_tiles(ref, x, n):
    for s in range(x.shape[1] // LANES):
        ref[pl.ds(s, n, stride=SUBLANES), :] = x[:, s * LANES:(s + 1) * LANES]


def _load_token_tiles(ref, n):
    return jnp.concatenate([ref[pl.ds(s, n, stride=SUBLANES), :] for s in range(SUBLANES)], axis=1)


def _split_bf16(a):
    hi = a.astype(BF16)
    return hi, (a - hi.astype(F32)).astype(BF16)


def _proj_kernel(x_ref, g_ref, w_ref, *refs, tm):
    outs = refs[:9]
    gb_ref, z_ref, pbuf = refs[9:]
    h = _rms(x_ref[...], g_ref[...]).astype(BF16)

    def proj(j):
        return jnp.dot(h, w_ref[:, j * ATTN_W:(j + 1) * ATTN_W], preferred_element_type=F32)

    for j in range(3):
        p = proj(j)
        if j == 0:
            p = p * (LOG2E * HEAD_DIM ** -0.5)
        outs[3 * j][0] = p.astype(BF16)
        for c in range(SLABS):
            pbuf[j, c] = p[:, c * LANES:(c + 1) * LANES]
        for di in (1, 2):
            dil = DILATIONS[di]
            o = outs[3 * j + di]
            for r in range(dil):
                for c in range(SLABS):
                    lo = r * ATTN_W + c * LANES
                    o[0, :, lo:lo + LANES] = pbuf[j, c, pl.ds(r, tm // dil, stride=dil), :].astype(BF16)
    gb_ref[...] = proj(3).astype(BF16)
    z_ref[...] = (proj(4) * proj(5)).astype(BF16)


def _proj(x2, g, w_in_bf, tm, batch, seq):
    t, d = x2.shape
    tps = seq // tm
    row = pl.BlockSpec((tm, ATTN_W), lambda i: (i, 0))
    view_specs, view_shapes = [], []
    for _ in range(3):
        for dil in DILATIONS:
            view_specs.append(pl.BlockSpec((1, tm // dil, dil * ATTN_W), lambda i: (i // tps, i % tps, 0)))
            view_shapes.append(jax.ShapeDtypeStruct((batch, seq // dil, dil * ATTN_W), BF16))
    flat = jax.ShapeDtypeStruct((t, ATTN_W), BF16)
    return pl.pallas_call(
        functools.partial(_proj_kernel, tm=tm),
        grid=(t // tm,),
        in_specs=[pl.BlockSpec((tm, d), lambda i: (i, 0)),
                  pl.BlockSpec((1, d), lambda i: (0, 0)),
                  pl.BlockSpec(w_in_bf.shape, lambda i: (0, 0))],
        out_specs=view_specs + [row, row],
        out_shape=view_shapes + [flat, flat],
        scratch_shapes=[pltpu.VMEM((3, SLABS, tm, LANES), F32)],
        compiler_params=_cparams("parallel"),
        name="proj",
    )(x2, g, w_in_bf)


def _bias_tables(dil):
    row = np.arange(Q_SUB)[:, None]
    col = np.arange(K_WIN)[None, :]
    rel = col - SIDE - row
    band = np.abs(rel) <= SIDE
    slopes = 2.0 ** (-8.0 * np.arange(1, N_HEADS + 1) / N_HEADS)
    tabs = []
    for t in range(4):
        ok = band.copy()
        if t & 1:
            ok &= col >= SIDE
        if t & 2:
            ok &= col < SIDE + Q_SUB
        for h in range(N_HEADS):
            tabs.append(np.where(ok, -slopes[h] * np.abs(rel) * dil * LOG2E, NEG))
    return jnp.asarray(np.stack(tabs), dtype=F32)


def _attn_kernel(q_ref, kp_ref, km_ref, kn_ref, vp_ref, vm_ref, vn_ref, bias_ref,
                 o_ref, st_ref, kbuf, vbuf, *, tq, n_sub_total):
    i = pl.program_id(2)
    pieces = ((0, SIDE, kp_ref, vp_ref), (SIDE, SIDE + tq, km_ref, vm_ref), (SIDE + tq, 2 * SIDE + tq, kn_ref, vn_ref))
    for lo_row, hi_row, k_ref, v_ref in pieces:
        kbuf[lo_row:hi_row] = k_ref[0]
        for hp in range(N_HEADS // 2):
            vbuf[lo_row:hi_row, 2 * hp * LANES:(2 * hp + 1) * LANES] = v_ref[0, :, hp * LANES:(hp + 1) * LANES]
    for hp in range(N_HEADS // 2):
        vbuf[:, (2 * hp + 1) * LANES:(2 * hp + 2) * LANES] = jnp.ones((tq + 2 * SIDE, LANES), BF16)

    lane = lax.broadcasted_iota(I32, (Q_SUB, LANES), 1)
    lo = lane < HEAD_DIM
    grp = lane // 16
    n_sub = tq // Q_SUB

    def sub(j, carry):
        r0 = pl.multiple_of(j * Q_SUB, Q_SUB)
        g_sub = i * n_sub + j
        tbl = (g_sub == 0).astype(I32) + 2 * (g_sub == n_sub_total - 1).astype(I32)
        m_tile = jnp.zeros((Q_SUB, LANES), F32)
        l_tile = jnp.zeros((Q_SUB, LANES), F32)
        for hp in range(N_HEADS // 2):
            cols = slice(hp * LANES, (hp + 1) * LANES)
            qp = q_ref[0, pl.ds(r0, Q_SUB), cols]
            kw = kbuf[pl.ds(r0, K_WIN), cols]
            vw = vbuf[pl.ds(r0, K_WIN), 2 * hp * LANES:(2 * hp + 2) * LANES]
            outs = []
            for par in range(2):
                qm = jnp.where(lo if par == 0 else jnp.logical_not(lo), qp, jnp.zeros_like(qp))
                s = lax.dot_general(qm, kw, (((1,), (1,)), ((), ())), preferred_element_type=F32)
                s = s + bias_ref[tbl * N_HEADS + 2 * hp + par]
                m = jnp.max(s, axis=1, keepdims=True)
                p = jnp.exp2(s - m).astype(BF16)
                pvl = jnp.dot(p, vw, preferred_element_type=F32)
                outs.append(pvl[:, :LANES])
                m_tile = jnp.where(grp == 2 * hp + par, m, m_tile)
                l_tile = jnp.where(grp == 2 * hp + par, pvl[:, LANES:], l_tile)
            o_ref[0, pl.ds(r0, Q_SUB), cols] = jnp.where(lo, outs[0], outs[1]).astype(BF16)
        st_ref[0, pl.ds(r0, Q_SUB), 0:LANES] = m_tile
        st_ref[0, pl.ds(r0, Q_SUB), LANES:2 * LANES] = l_tile
        return carry

    lax.fori_loop(0, n_sub, sub, 0, unroll=True)


def _attn_branch(qv, kv, vv, dil):
    batch, sd, _ = qv.shape
    tq = min(512, sd)
    n_tiles = sd // tq
    bias = _bias_tables(dil)
    r_side = tq // SIDE
    last_side = sd // SIDE - 1

    main = pl.BlockSpec((1, tq, ATTN_W), lambda b, r, i: (b, i, r))
    prev = pl.BlockSpec((1, SIDE, ATTN_W), lambda b, r, i: (b, jnp.maximum(i * r_side - 1, 0), r))
    nxt = pl.BlockSpec((1, SIDE, ATTN_W), lambda b, r, i: (b, jnp.minimum((i + 1) * r_side, last_side), r))
    st_spec = pl.BlockSpec((1, tq, 2 * LANES), lambda b, r, i: (b, i, r))
    return pl.pallas_call(
        functools.partial(_attn_kernel, tq=tq, n_sub_total=sd // Q_SUB),
        grid=(batch, dil, n_tiles),
        in_specs=[main, prev, main, nxt, prev, main, nxt,
                  pl.BlockSpec(bias.shape, lambda b, r, i: (0, 0, 0))],
        out_specs=[main, st_spec],
        out_shape=[jax.ShapeDtypeStruct((batch, sd, dil * ATTN_W), BF16),
                   jax.ShapeDtypeStruct((batch, sd, dil * 2 * LANES), F32)],
        scratch_shapes=[pltpu.VMEM((tq + 2 * SIDE, ATTN_W), BF16),
                        pltpu.VMEM((tq + 2 * SIDE, 2 * ATTN_W), BF16)],
        compiler_params=_cparams("parallel", "parallel", "parallel"),
        name=f"attn_d{dil}",
    )(qv, kv, kv, kv, vv, vv, vv, bias)


def _mix_kernel(x_ref, o1_ref, o4_ref, o16_ref, l1_ref, l4_ref, l16_ref, gb_ref, z_ref, zp_ref, zn_ref,
                cw_ref, ga_ref, gc_ref, wo_ref, gf_ref, wrh_ref, wrl_ref, br_ref, upper_ref, expand_ref,
                x1_ref, h2_ref, ri_ref, rg_ref, cnt_ref, onat, lnat, carry, *, tm, tiles_per_seq):
    i = pl.program_id(0)

    @pl.when(i == 0)
    def _():
        carry[...] = jnp.zeros_like(carry)

    for bi, (o_ref, l_ref, dil) in enumerate(((o4_ref, l4_ref, 4), (o16_ref, l16_ref, 16))):
        n = tm // dil
        for r in range(dil):
            for c in range(2):
                lo = (2 * r + c) * LANES
                lnat[bi, c, pl.ds(r, n, stride=dil), :] = l_ref[0, :, lo:lo + LANES]
            for c in range(SLABS):
                lo = r * ATTN_W + c * LANES
                onat[bi, c, pl.ds(r, n, stride=dil), :] = o_ref[0, :, lo:lo + LANES].astype(F32)

    ms = [l1_ref[0, :, 0:LANES], lnat[0, 0], lnat[1, 0]]
    ls = [l1_ref[0, :, LANES:2 * LANES], lnat[0, 1], lnat[1, 1]]
    mx = jnp.maximum(jnp.maximum(ms[0], ms[1]), ms[2])
    w = [jnp.exp2(t - mx) for t in ms]
    inv = 1.0 / (w[0] * ls[0] + w[1] * ls[1] + w[2] * ls[2])
    wide = []
    for g in range(3):
        hi, lo_part = _split_bf16(w[g] * inv)
        wide.append(jnp.dot(hi, expand_ref[...], preferred_element_type=F32)
                    + jnp.dot(lo_part, expand_ref[...], preferred_element_type=F32))
    slabs = []
    for c in range(SLABS):
        cs = slice(c * LANES, (c + 1) * LANES)
        slabs.append(wide[0][:, cs] * o1_ref[0, :, cs].astype(F32)
                     + wide[1][:, cs] * onat[0, c] + wide[2][:, cs] * onat[1, c])
    attn = jnp.concatenate(slabs, axis=1)

    z = z_ref[...].astype(F32)
    row = lax.broadcasted_iota(I32, (tm, 1), 0)
    seq_first = (i % tiles_per_seq) == 0
    seq_last = (i % tiles_per_seq) == tiles_per_seq - 1
    z_before = jnp.where(seq_first, 0.0, zp_ref[15:16, :].astype(F32))
    z_after = jnp.where(seq_last, 0.0, zn_ref[0:1, :].astype(F32))
    z_up = jnp.where(row == 0, z_before, pltpu.roll(z, 1, 0))
    z_dn = jnp.where(row == tm - 1, z_after, pltpu.roll(z, tm - 1, 0))
    conv = gb_ref[...].astype(F32) * (cw_ref[0:1, :] * z_up + cw_ref[1:2, :] * z + cw_ref[2:3, :] * z_dn)

    na = _rms(attn, ga_ref[...]).astype(BF16)
    nc = _rms(conv, gc_ref[...]).astype(BF16)
    mix = (jnp.dot(na, wo_ref[0:ATTN_W, :], preferred_element_type=F32)
           + jnp.dot(nc, wo_ref[ATTN_W:, :], preferred_element_type=F32))
    x1 = x_ref[...] + mix
    x1_ref[...] = x1
    h2 = _rms(x1, gf_ref[...])
    _store_token_tiles(h2_ref, h2, tm)

    hh, hl = _split_bf16(h2)
    logits = (jnp.dot(hh, wrh_ref[...], preferred_element_type=F32)
              + jnp.dot(hl, wrh_ref[...], preferred_element_type=F32)
              + jnp.dot(hh, wrl_ref[...], preferred_element_type=F32) + br_ref[...])
    lt = logits.T[0:N_EXPERTS, :]

    erow = lax.broadcasted_iota(I32, (N_EXPERTS, tm), 0)
    work = lt
    idx, val = [], []
    for _ in range(TOP_K):
        mk = jnp.max(work, axis=0, keepdims=True)
        ik = jnp.min(jnp.where(work == mk, erow, N_EXPERTS), axis=0, keepdims=True)
        idx.append(ik)
        val.append(mk)
        work = jnp.where(erow == ik, 2 * NEG, work)
    ex = [jnp.exp(v - val[0]) for v in val]
    ginv = 1.0 / (ex[0] + ex[1] + ex[2] + ex[3])
    gates = [e * ginv for e in ex]

    hit = [erow == ik for ik in idx]
    onehot = (hit[0] | hit[1] | hit[2] | hit[3]).astype(F32)
    before = jnp.dot(onehot.astype(BF16), upper_ref[...], preferred_element_type=F32) + carry[:, 0:1]
    ranks = [jnp.sum(jnp.where(h, before, 0.0), axis=0, keepdims=True) for h in hit]
    new_carry = carry[:, 0:1] + jnp.sum(onehot, axis=1, keepdims=True)
    carry[...] = jnp.broadcast_to(new_carry, carry.shape)
    cnt_ref[...] = jnp.broadcast_to(new_carry, cnt_ref.shape)

    ri_ref[...] = jnp.concatenate(idx + [r.astype(I32) for r in ranks], axis=0)
    rg_ref[...] = jnp.concatenate(gates + [jnp.zeros((TOP_K, tm), F32)], axis=0)


def _mix(x2, o_views, l_views, gb, z, conv_w, ga, gc, w_out_bf, gf, w_router, b_router, tm, batch, seq):
    t, d = x2.shape
    n = t // tm
    tps = seq // tm
    upper = jnp.asarray(np.triu(np.ones((tm, tm), np.float32), 1), dtype=BF16)
    expand = jnp.asarray(np.arange(LANES)[:, None] == 16 * (np.arange(ATTN_W)[None, :] // HEAD_DIM), dtype=BF16)
    wr_pad = jnp.zeros((d, LANES), F32).at[:, :N_EXPERTS].set(w_router)
    wr_hi, wr_lo = _split_bf16(wr_pad)
    br_pad = jnp.zeros((1, LANES), F32).at[0, :N_EXPERTS].set(b_router)

    rowd = pl.BlockSpec((tm, d), lambda i: (i, 0))
    rowa = pl.BlockSpec((tm, ATTN_W), lambda i: (i, 0))
    halo = tm // 16
    zprev = pl.BlockSpec((16, CONV_W), lambda i: (jnp.maximum(i * halo - 1, 0), 0))
    znext = pl.BlockSpec((16, CONV_W), lambda i: (jnp.minimum((i + 1) * halo, t // 16 - 1), 0))

    def view(width, dil):
        return pl.BlockSpec((1, tm // dil, dil * width), lambda i: (i // tps, i % tps, 0))

    def full(a):
        return pl.BlockSpec(a.shape, lambda i: (0,) * a.ndim)

    consts = (conv_w, ga, gc, w_out_bf, gf, wr_hi, wr_lo, br_pad, upper, expand)
    return pl.pallas_call(
        functools.partial(_mix_kernel, tm=tm, tiles_per_seq=tps),
        grid=(n,),
        in_specs=[rowd] + [view(ATTN_W, dil) for dil in DILATIONS] + [view(2 * LANES, dil) for dil in DILATIONS]
                 + [rowa, rowa, zprev, znext] + [full(a) for a in consts],
        out_specs=[rowd, pl.BlockSpec((tm * SUBLANES, LANES), lambda i: (i, 0)),
                   pl.BlockSpec((8, tm), lambda i: (0, i)), pl.BlockSpec((8, tm), lambda i: (0, i)),
                   pl.BlockSpec((N_EXPERTS, LANES), lambda i: (0, 0))],
        out_shape=[jax.ShapeDtypeStruct((t, d), F32), jax.ShapeDtypeStruct((t * SUBLANES, LANES), F32),
                   jax.ShapeDtypeStruct((8, t), I32), jax.ShapeDtypeStruct((8, t), F32),
                   jax.ShapeDtypeStruct((N_EXPERTS, LANES), F32)],
        scratch_shapes=[pltpu.VMEM((2, SLABS, tm, LANES), F32), pltpu.VMEM((2, 2, tm, LANES), F32),
                        pltpu.VMEM((N_EXPERTS, LANES), F32)],
        compiler_params=_cparams("arbitrary"),
        name="mix_router",
    )(x2, *o_views, *l_views, gb, z, z, z, *consts)


def _dispatch_kernel(zrow_ref, dest_ref, h_ref, xs_ref, slot_smem, zero_buf, sem_idx, sem_row, sem_zero,
                     *, tm, tb, n_tiles):
    i = pl.program_id(0)
    n = tm * TOP_K

    def idx_copy(tile, buf):
        return pltpu.make_async_copy(dest_ref.at[pl.ds(tile * n, n)], slot_smem.at[buf], sem_idx.at[buf])

    @pl.when(i == 0)
    def _():
        zero_buf[...] = jnp.zeros_like(zero_buf)
        for j in range(2 * N_EXPERTS):
            @pl.when(zrow_ref[j] >= 0)
            def _():
                start = pl.multiple_of(zrow_ref[j] * SUBLANES, tb * SUBLANES)
                pltpu.make_async_copy(zero_buf, xs_ref.at[pl.ds(start, tb * SUBLANES)], sem_zero).start()
        for j in range(2 * N_EXPERTS):
            @pl.when(zrow_ref[j] >= 0)
            def _():
                pltpu.make_async_copy(zero_buf, xs_ref.at[pl.ds(0, tb * SUBLANES)], sem_zero).wait()
        idx_copy(0, 0).start()

    cur = i % 2
    idx_copy(i, cur).wait()

    @pl.when(i + 1 < n_tiles)
    def _():
        idx_copy(i + 1, 1 - cur).start()

    def body(t, c):
        src = h_ref.at[pl.ds(pl.multiple_of(t * SUBLANES, SUBLANES), SUBLANES)]
        for k in range(TOP_K):
            row = pl.multiple_of(slot_smem[cur, k * tm + t] * SUBLANES, SUBLANES)
            pltpu.make_async_copy(src, xs_ref.at[pl.ds(row, SUBLANES)], sem_row).start(priority=k % 2)
        return c

    lax.fori_loop(0, tm, body, 0, unroll=8)

    for _ in range(TOP_K):
        pltpu.make_async_copy(h_ref, xs_ref.at[pl.ds(0, tm * SUBLANES)], sem_row).wait()


def _dispatch(zrow, dest_tiles, h2_tiles, n_slots, tm, tb):
    t = h2_tiles.shape[0] // SUBLANES
    n_tiles = t // tm
    grid_spec = pltpu.PrefetchScalarGridSpec(
        num_scalar_prefetch=1,
        grid=(n_tiles,),
        in_specs=[pl.BlockSpec(memory_space=pl.ANY),
                  pl.BlockSpec((tm * SUBLANES, LANES), lambda i, zr: (i, 0))],
        out_specs=pl.BlockSpec(memory_space=pl.ANY),
        scratch_shapes=[pltpu.SMEM((2, tm * TOP_K), I32), pltpu.VMEM((tb * SUBLANES, LANES), F32),
                        pltpu.SemaphoreType.DMA((2,)), pltpu.SemaphoreType.DMA, pltpu.SemaphoreType.DMA],
    )
    return pl.pallas_call(
        functools.partial(_dispatch_kernel, tm=tm, tb=tb, n_tiles=n_tiles),
        grid_spec=grid_spec,
        out_shape=jax.ShapeDtypeStruct((n_slots * SUBLANES, LANES), F32),
        compiler_params=_cparams("arbitrary"),
        name="dispatch",
    )(zrow, dest_tiles, h2_tiles)


def _expert_kernel(be_ref, bx_ref, bv_ref, xs_ref, wgu_ref, bgu_ref, wd_ref, bd_ref, o_ref, wgu_bf, wd_bf,
                   *, tb, dff):
    i = pl.program_id(0)

    @pl.when(jnp.logical_or(i == 0, be_ref[i] != be_ref[jnp.maximum(i - 1, 0)]))
    def _():
        wgu_bf[...] = wgu_ref[0].astype(BF16)
        wd_bf[...] = wd_ref[0].astype(BF16)

    @pl.when(bv_ref[i] > 0)
    def _():
        x = _load_token_tiles(xs_ref, tb).astype(BF16)
        hu = jnp.dot(x, wgu_bf[...], preferred_element_type=F32) + bgu_ref[0]
        a = jnp.minimum(hu[:, :dff], SWIGLU_LIMIT)
        lin = jnp.clip(hu[:, dff:], -SWIGLU_LIMIT, SWIGLU_LIMIT)
        act = a * (1.0 / (1.0 + jnp.exp(-SWIGLU_ALPHA * a))) * (lin + 1.0)
        out = jnp.dot(act.astype(BF16), wd_bf[...], preferred_element_type=F32) + bd_ref[0]
        _store_token_tiles(o_ref, out, tb)

    @pl.when(bv_ref[i] == 0)
    def _():
        o_ref[...] = jnp.zeros_like(o_ref)


def _experts(blk_e, blk_x, blk_v, xs_tiles, wgu, bgu, wd, bd, tb):
    d, dff = wd.shape[2], wd.shape[1]
    n_slots = xs_tiles.shape[0] // SUBLANES
    nb = n_slots // tb
    rows = tb * SUBLANES
    grid_spec = pltpu.PrefetchScalarGridSpec(
        num_scalar_prefetch=3,
        grid=(nb,),
        in_specs=[pl.BlockSpec((rows, LANES), lambda i, be, bx, bv: (bx[i], 0)),
                  pl.BlockSpec((1, d, 2 * dff), lambda i, be, bx, bv: (be[i], 0, 0)),
                  pl.BlockSpec((1, 1, 2 * dff), lambda i, be, bx, bv: (be[i], 0, 0)),
                  pl.BlockSpec((1, dff, d), lambda i, be, bx, bv: (be[i], 0, 0)),
                  pl.BlockSpec((1, 1, d), lambda i, be, bx, bv: (be[i], 0, 0))],
        out_specs=pl.BlockSpec((rows, LANES), lambda i, be, bx, bv: (i, 0)),
        scratch_shapes=[pltpu.VMEM((d, 2 * dff), BF16), pltpu.VMEM((dff, d), BF16)],
    )
    return pl.pallas_call(
        functools.partial(_expert_kernel, tb=tb, dff=dff),
        grid_spec=grid_spec,
        out_shape=jax.ShapeDtypeStruct((n_slots * SUBLANES, LANES), F32),
        compiler_params=_cparams("arbitrary"),
        name="experts",
    )(blk_e, blk_x, blk_v, xs_tiles, wgu, bgu, wd, bd)


def _combine_kernel(dest_ref, x1_ref, rg_ref, fg_ref, mo_ref, out_ref, slot_smem, gbuf, sem_idx, sem_row,
                    *, tm, n_tiles):
    i = pl.program_id(0)
    n = tm * TOP_K

    def idx_copy(tile):
        buf = tile % 3
        return pltpu.make_async_copy(dest_ref.at[pl.ds(tile * n, n)], slot_smem.at[buf], sem_idx.at[buf])

    def issue(tile):
        sb, gb = tile % 3, tile % 2

        def body(t, c):
            dst = pl.ds(pl.multiple_of(t * SUBLANES, SUBLANES), SUBLANES)
            for k in range(TOP_K):
                row = pl.multiple_of(slot_smem[sb, k * tm + t] * SUBLANES, SUBLANES)
                pltpu.make_async_copy(mo_ref.at[pl.ds(row, SUBLANES)], gbuf.at[gb, k, dst],
                                      sem_row.at[gb]).start(priority=k % 2)
            return c

        lax.fori_loop(0, tm, body, 0, unroll=8)

    @pl.when(i == 0)
    def _():
        idx_copy(0).start()
        if n_tiles > 1:
            idx_copy(1).start()
        idx_copy(0).wait()
        issue(0)

    @pl.when(i + 1 < n_tiles)
    def _():
        idx_copy(i + 1).wait()
        issue(i + 1)

    @pl.when(i + 2 < n_tiles)
    def _():
        idx_copy(i + 2).start()

    cur = i % 2
    for k in range(TOP_K):
        pltpu.make_async_copy(mo_ref.at[pl.ds(0, tm * SUBLANES)], gbuf.at[cur, k], sem_row.at[cur]).wait()

    g_rows = jnp.concatenate([rg_ref[...], jnp.zeros((LANES - 8, tm), F32)], axis=0)
    g = g_rows.T
    y = x1_ref[...]
    for k in range(TOP_K):
        y = y + g[:, k:k + 1] * _load_token_tiles(gbuf.at[cur, k], tm)
    out_ref[...] = _rms(y, fg_ref[...])


def _combine(dest_tiles, x1, rg, fg, mlp_out, tm):
    t, d = x1.shape
    n_tiles = t // tm
    rowd = pl.BlockSpec((tm, d), lambda i: (i, 0))
    return pl.pallas_call(
        functools.partial(_combine_kernel, tm=tm, n_tiles=n_tiles),
        grid=(n_tiles,),
        in_specs=[pl.BlockSpec(memory_space=pl.ANY), rowd,
                  pl.BlockSpec((8, tm), lambda i: (0, i)),
                  pl.BlockSpec((1, d), lambda i: (0, 0)),
                  pl.BlockSpec(memory_space=pl.ANY)],
        out_specs=rowd,
        out_shape=jax.ShapeDtypeStruct((t, d), F32),
        scratch_shapes=[pltpu.SMEM((3, tm * TOP_K), I32), pltpu.VMEM((2, TOP_K, tm * SUBLANES, LANES), F32),
                        pltpu.SemaphoreType.DMA((3,)), pltpu.SemaphoreType.DMA((2,))],
        compiler_params=_cparams("arbitrary"),
        name="combine",
    )(dest_tiles, x1, rg, fg, mlp_out)


def _block_tables(counts, idx, rank, tb, nb):
    padded = (counts + tb - 1) // tb * tb
    pad_end = jnp.cumsum(padded)
    pad_start = pad_end - padded
    sel = idx[..., None] == jnp.arange(N_EXPERTS, dtype=I32)
    dest = jnp.sum(jnp.where(sel, pad_start, 0), axis=-1) + rank
    blocks_e = padded // tb
    blk_end = jnp.cumsum(blocks_e)
    total = blk_end[-1]
    bid = jnp.arange(nb, dtype=I32)
    src = jnp.minimum(bid, total - 1)
    e = jnp.minimum(jnp.sum((src[:, None] >= blk_end[None, :]).astype(I32), axis=1), N_EXPERTS - 1)
    first = jnp.sum(jnp.where(e[:, None] == jnp.arange(N_EXPERTS), blk_end - blocks_e, 0), axis=1)
    cnt = jnp.sum(jnp.where(e[:, None] == jnp.arange(N_EXPERTS), counts, 0), axis=1)
    valid = jnp.where(bid < total, jnp.clip(cnt - (src - first) * tb, 0, tb), 0)
    tail = total + jnp.arange(N_EXPERTS, dtype=I32)
    zrow = jnp.concatenate([jnp.where(padded > 0, pad_end - tb, -1), jnp.where(tail < nb, tail * tb, -1)])
    return dest.astype(I32), e.astype(I32), src.astype(I32), valid.astype(I32), zrow.astype(I32)


def _tile_major(dest, tm):
    k, t = dest.shape
    return dest.reshape(k, t // tm, tm).transpose(1, 0, 2).reshape(-1)


def kernel(x, mix_norm_g, w_in, conv_w, attn_out_norm_g, conv_out_norm_g, w_out, ffn_norm_g, w_router,
           b_router, w_gate_up, b_gate_up, w_down, b_down, final_norm_g):
    batch, seq, d = x.shape
    t = batch * seq
    assert w_in.shape[0] == 1, "single-layer trunk: the final RMSNorm is fused into the combine kernel"
    assert d == SUBLANES * LANES, "one token row must fill exactly one (8, 128) f32 tile"
    tm = 512
    tb = 512
    tc = 256
    nb = (t * TOP_K) // tb + N_EXPERTS
    x2 = x.reshape(t, d)

    *qkv, gb, z = _proj(x2, mix_norm_g[0][None, :], w_in[0].astype(BF16), tm, batch, seq)
    o_views, l_views = [], []
    for di, dil in enumerate(DILATIONS):
        o, lse = _attn_branch(qkv[di], qkv[3 + di], qkv[6 + di], dil)
        o_views.append(o)
        l_views.append(lse)

    x1, h2, ri, rg, cnt = _mix(x2, o_views, l_views, gb, z, conv_w[0], attn_out_norm_g[0][None, :],
                               conv_out_norm_g[0][None, :], w_out[0].astype(BF16), ffn_norm_g[0][None, :],
                               w_router[0], b_router[0], tm, batch, seq)

    counts = cnt[:, 0].astype(I32)
    dest, blk_e, blk_x, blk_v, zrow = _block_tables(counts, ri[:TOP_K], ri[TOP_K:], tb, nb)

    xs = _dispatch(zrow, _tile_major(dest, tm), h2, nb * tb, tm, tb)
    mo = _experts(blk_e, blk_x, blk_v, xs, w_gate_up[0], b_gate_up[0][:, None, :],
                  w_down[0], b_down[0][:, None, :], tb)
    out = _combine(_tile_major(dest, tc), x1, rg, final_norm_g[None, :], mo, tc)
    return out.reshape(batch, seq, d)
```

```python
import functools

import numpy as np
import jax
import jax.numpy as jnp
from jax import lax
from jax.experimental import pallas as pl
from jax.experimental.pallas import tpu as pltpu

F32 = jnp.float32
BF16 = jnp.bfloat16
I32 = jnp.int32

HEAD_DIM = 64
N_HEADS = 8
ATTN_W = N_HEADS * HEAD_DIM
CONV_W = 512
N_EXPERTS = 32
TOP_K = 4
DILATIONS = (1, 4, 16)
SIDE = 64
SWIGLU_ALPHA = 1.702
SWIGLU_LIMIT = 7.0
RMS_EPS = 1e-5
NEG = -1e30
LOG2E = 1.4426950408889634

LANES = 128
SLABS = ATTN_W // LANES
Q_SUB = 128
K_WIN = Q_SUB + 2 * SIDE
VMEM_LIMIT = 56 * 1024 * 1024


def _cparams(*sem):
    return pltpu.CompilerParams(dimension_semantics=sem, vmem_limit_bytes=VMEM_LIMIT)


def _rms(x, g):
    return x * lax.rsqrt(jnp.mean(x * x, axis=-1, keepdims=True) + RMS_EPS) * g


SUBLANES = 8


def _store_token_tiles(ref, x, n):
    for s in range(x.shape[1] // LANES):
        ref[pl.ds(s, n, stride=SUBLANES), :] = x[:, s * LANES:(s + 1) * LANES]


def _load_token_tiles(ref, n):
    return jnp.concatenate([ref[pl.ds(s, n, stride=SUBLANES), :] for s in range(SUBLANES)], axis=1)


def _split_bf16(a):
    hi = a.astype(BF16)
    return hi, (a - hi.astype(F32)).astype(BF16)


def _proj_kernel(x_ref, g_ref, w_ref, *refs, tm):
    outs = refs[:9]
    gb_ref, z_ref, pbuf = refs[9:]
    h = _rms(x_ref[...], g_ref[...]).astype(BF16)

    def proj(j):
        return jnp.dot(h, w_ref[:, j * ATTN_W:(j + 1) * ATTN_W], preferred_element_type=F32)

    for j in range(3):
        p = proj(j)
        if j == 0:
            p = p * (LOG2E * HEAD_DIM ** -0.5)
        outs[3 * j][0] = p.astype(BF16)
        for c in range(SLABS):
            pbuf[j, c] = p[:, c * LANES:(c + 1) * LANES]
        for di in (1, 2):
            dil = DILATIONS[di]
            o = outs[3 * j + di]
            for r in range(dil):
                for c in range(SLABS):
                    lo = r * ATTN_W + c * LANES
                    o[0, :, lo:lo + LANES] = pbuf[j, c, pl.ds(r, tm // dil, stride=dil), :].astype(BF16)
    gb_ref[...] = proj(3).astype(BF16)
    z_ref[...] = (proj(4) * proj(5)).astype(BF16)


def _proj(x2, g, w_in_bf, tm, batch, seq):
    t, d = x2.shape
    tps = seq // tm
    row = pl.BlockSpec((tm, ATTN_W), lambda i: (i, 0))
    view_specs, view_shapes = [], []
    for _ in range(3):
        for dil in DILATIONS:
            view_specs.append(pl.BlockSpec((1, tm // dil, dil * ATTN_W), lambda i: (i // tps, i % tps, 0)))
            view_shapes.append(jax.ShapeDtypeStruct((batch, seq // dil, dil * ATTN_W), BF16))
    flat = jax.ShapeDtypeStruct((t, ATTN_W), BF16)
    return pl.pallas_call(
        functools.partial(_proj_kernel, tm=tm),
        grid=(t // tm,),
        in_specs=[pl.BlockSpec((tm, d), lambda i: (i, 0)),
                  pl.BlockSpec((1, d), lambda i: (0, 0)),
                  pl.BlockSpec(w_in_bf.shape, lambda i: (0, 0))],
        out_specs=view_specs + [row, row],
        out_shape=view_shapes + [flat, flat],
        scratch_shapes=[pltpu.VMEM((3, SLABS, tm, LANES), F32)],
        compiler_params=_cparams("parallel"),
        name="proj",
    )(x2, g, w_in_bf)


def _bias_tables(dil):
    row = np.arange(Q_SUB)[:, None]
    col = np.arange(K_WIN)[None, :]
    rel = col - SIDE - row
    band = np.abs(rel) <= SIDE
    slopes = 2.0 ** (-8.0 * np.arange(1, N_HEADS + 1) / N_HEADS)
    tabs = []
    for t in range(4):
        ok = band.copy()
        if t & 1:
            ok &= col >= SIDE
        if t & 2:
            ok &= col < SIDE + Q_SUB
        for h in range(N_HEADS):
            tabs.append(np.where(ok, -slopes[h] * np.abs(rel) * dil * LOG2E, NEG))
    return jnp.asarray(np.stack(tabs), dtype=F32)


def _attn_kernel(q_ref, kp_ref, km_ref, kn_ref, vp_ref, vm_ref, vn_ref, bias_ref,
                 o_ref, st_ref, kbuf, vbuf, *, tq, n_sub_total):
    i = pl.program_id(2)
    pieces = ((0, SIDE, kp_ref, vp_ref), (SIDE, SIDE + tq, km_ref, vm_ref), (SIDE + tq, 2 * SIDE + tq, kn_ref, vn_ref))
    for lo_row, hi_row, k_ref, v_ref in pieces:
        kbuf[lo_row:hi_row] = k_ref[0]
        for hp in range(N_HEADS // 2):
            vbuf[lo_row:hi_row, 2 * hp * LANES:(2 * hp + 1) * LANES] = v_ref[0, :, hp * LANES:(hp + 1) * LANES]

    @pl.when((pl.program_id(0) == 0) & (pl.program_id(1) == 0) & (i == 0))
    def _():
        for hp in range(N_HEADS // 2):
            vbuf[:, (2 * hp + 1) * LANES:(2 * hp + 2) * LANES] = jnp.ones((tq + 2 * SIDE, LANES), BF16)

    lane = lax.broadcasted_iota(I32, (Q_SUB, LANES), 1)
    lo = lane < HEAD_DIM
    grp = lane // 16
    n_sub = tq // Q_SUB

    def sub(j, carry):
        r0 = pl.multiple_of(j * Q_SUB, Q_SUB)
        g_sub = i * n_sub + j
        tbl = (g_sub == 0).astype(I32) + 2 * (g_sub == n_sub_total - 1).astype(I32)
        m_tile = jnp.zeros((Q_SUB, LANES), F32)
        l_tile = jnp.zeros((Q_SUB, LANES), F32)
        for hp in range(N_HEADS // 2):
            cols = slice(hp * LANES, (hp + 1) * LANES)
            qp = q_ref[0, pl.ds(r0, Q_SUB), cols]
            kw = kbuf[pl.ds(r0, K_WIN), cols]
            vw = vbuf[pl.ds(r0, K_WIN), 2 * hp * LANES:(2 * hp + 2) * LANES]
            outs = []
            for par in range(2):
                qm = jnp.where(lo if par == 0 else jnp.logical_not(lo), qp, jnp.zeros_like(qp))
                s = lax.dot_general(qm, kw, (((1,), (1,)), ((), ())), preferred_element_type=F32)
                s = s + bias_ref[tbl * N_HEADS + 2 * hp + par]
                m = jnp.max(s, axis=1, keepdims=True)
                p = jnp.exp2(s - m).astype(BF16)
                pvl = jnp.dot(p, vw, preferred_element_type=F32)
                outs.append(pvl[:, :LANES])
                m_tile = jnp.where(grp == 2 * hp + par, m, m_tile)
                l_tile = jnp.where(grp == 2 * hp + par, pvl[:, LANES:], l_tile)
            o_ref[0, pl.ds(r0, Q_SUB), cols] = jnp.where(lo, outs[0], outs[1]).astype(BF16)
        st_ref[0, pl.ds(r0, Q_SUB), 0:LANES] = m_tile
        st_ref[0, pl.ds(r0, Q_SUB), LANES:2 * LANES] = l_tile
        return carry

    lax.fori_loop(0, n_sub, sub, 0, unroll=min(n_sub, 4))


def _attn_branch(qv, kv, vv, dil):
    batch, sd, _ = qv.shape
    tq = min(1024, sd)
    n_tiles = sd // tq
    bias = _bias_tables(dil)
    r_side = tq // SIDE
    last_side = sd // SIDE - 1

    main = pl.BlockSpec((1, tq, ATTN_W), lambda b, r, i: (b, i, r))
    prev = pl.BlockSpec((1, SIDE, ATTN_W), lambda b, r, i: (b, jnp.maximum(i * r_side - 1, 0), r))
    nxt = pl.BlockSpec((1, SIDE, ATTN_W), lambda b, r, i: (b, jnp.minimum((i + 1) * r_side, last_side), r))
    st_spec = pl.BlockSpec((1, tq, 2 * LANES), lambda b, r, i: (b, i, r))
    return pl.pallas_call(
        functools.partial(_attn_kernel, tq=tq, n_sub_total=sd // Q_SUB),
        grid=(batch, dil, n_tiles),
        in_specs=[main, prev, main, nxt, prev, main, nxt,
                  pl.BlockSpec(bias.shape, lambda b, r, i: (0, 0, 0))],
        out_specs=[main, st_spec],
        out_shape=[jax.ShapeDtypeStruct((batch, sd, dil * ATTN_W), BF16),
                   jax.ShapeDtypeStruct((batch, sd, dil * 2 * LANES), F32)],
        scratch_shapes=[pltpu.VMEM((tq + 2 * SIDE, ATTN_W), BF16),
                        pltpu.VMEM((tq + 2 * SIDE, 2 * ATTN_W), BF16)],
        compiler_params=_cparams("arbitrary", "arbitrary", "arbitrary"),
        name=f"attn_d{dil}",
    )(qv, kv, kv, kv, vv, vv, vv, bias)


def _mix_kernel(x_ref, o1_ref, o4_ref, o16_ref, l1_ref, l4_ref, l16_ref, gb_ref, z_ref, zp_ref, zn_ref,
                cw_ref, ga_ref, gc_ref, wo_ref, gf_ref, wr2_ref, br_ref, upper_ref, expand_ref,
                x1_ref, h2_ref, ri_ref, rg_ref, cnt_ref, onat, lnat, carry, *, tm, tiles_per_seq):
    i = pl.program_id(0)

    @pl.when(i == 0)
    def _():
        carry[...] = jnp.zeros_like(carry)

    for bi, (o_ref, l_ref, dil) in enumerate(((o4_ref, l4_ref, 4), (o16_ref, l16_ref, 16))):
        n = tm // dil
        for r in range(dil):
            for c in range(2):
                lo = (2 * r + c) * LANES
                lnat[bi, c, pl.ds(r, n, stride=dil), :] = l_ref[0, :, lo:lo + LANES]
            for c in range(SLABS):
                lo = r * ATTN_W + c * LANES
                onat[bi, c, pl.ds(r, n, stride=dil), :] = o_ref[0, :, lo:lo + LANES].astype(F32)

    ms = [l1_ref[0, :, 0:LANES], lnat[0, 0], lnat[1, 0]]
    ls = [l1_ref[0, :, LANES:2 * LANES], lnat[0, 1], lnat[1, 1]]
    mx = jnp.maximum(jnp.maximum(ms[0], ms[1]), ms[2])
    w = [jnp.exp2(t - mx) for t in ms]
    inv = 1.0 / (w[0] * ls[0] + w[1] * ls[1] + w[2] * ls[2])
    wide = []
    for g in range(3):
        hi_lo = jnp.concatenate(_split_bf16(w[g] * inv), axis=1)
        wide.append(jnp.dot(hi_lo, expand_ref[...], preferred_element_type=F32))
    slabs = []
    for c in range(SLABS):
        cs = slice(c * LANES, (c + 1) * LANES)
        slabs.append(wide[0][:, cs] * o1_ref[0, :, cs].astype(F32)
                     + wide[1][:, cs] * onat[0, c] + wide[2][:, cs] * onat[1, c])
    attn = jnp.concatenate(slabs, axis=1)

    z = z_ref[...].astype(F32)
    row = lax.broadcasted_iota(I32, (tm, 1), 0)
    seq_first = (i % tiles_per_seq) == 0
    seq_last = (i % tiles_per_seq) == tiles_per_seq - 1
    z_before = jnp.where(seq_first, 0.0, zp_ref[15:16, :].astype(F32))
    z_after = jnp.where(seq_last, 0.0, zn_ref[0:1, :].astype(F32))
    z_up = jnp.where(row == 0, z_before, pltpu.roll(z, 1, 0))
    z_dn = jnp.where(row == tm - 1, z_after, pltpu.roll(z, tm - 1, 0))
    conv = gb_ref[...].astype(F32) * (cw_ref[0:1, :] * z_up + cw_ref[1:2, :] * z + cw_ref[2:3, :] * z_dn)

    na = _rms(attn, ga_ref[...]).astype(BF16)
    nc = _rms(conv, gc_ref[...]).astype(BF16)
    mix = (jnp.dot(na, wo_ref[0:ATTN_W, :], preferred_element_type=F32)
           + jnp.dot(nc, wo_ref[ATTN_W:, :], preferred_element_type=F32))
    x1 = x_ref[...] + mix
    x1_ref[...] = x1
    h2 = _rms(x1, gf_ref[...])
    _store_token_tiles(h2_ref, h2, tm)

    hh, hl = _split_bf16(h2)
    both = jnp.dot(hh, wr2_ref[...], preferred_element_type=F32)
    logits = (both[:, :LANES] + both[:, LANES:]
              + jnp.dot(hl, wr2_ref[:, :LANES], preferred_element_type=F32) + br_ref[...])
    lt = logits.T[0:N_EXPERTS, :]

    erow = lax.broadcasted_iota(I32, (N_EXPERTS, tm), 0)
    work = lt
    idx, val = [], []
    for _ in range(TOP_K):
        mk = jnp.max(work, axis=0, keepdims=True)
        ik = jnp.min(jnp.where(work == mk, erow, N_EXPERTS), axis=0, keepdims=True)
        idx.append(ik)
        val.append(mk)
        work = jnp.where(erow == ik, 2 * NEG, work)
    ex = [jnp.exp(v - val[0]) for v in val]
    ginv = 1.0 / (ex[0] + ex[1] + ex[2] + ex[3])
    gates = [e * ginv for e in ex]

    hit = [erow == ik for ik in idx]
    onehot = (hit[0] | hit[1] | hit[2] | hit[3]).astype(F32)
    before = jnp.dot(onehot.astype(BF16), upper_ref[...], preferred_element_type=F32) + carry[:, 0:1]
    ranks = [jnp.sum(jnp.where(h, before, 0.0), axis=0, keepdims=True) for h in hit]
    new_carry = carry[:, 0:1] + jnp.sum(onehot, axis=1, keepdims=True)
    carry[...] = jnp.broadcast_to(new_carry, carry.shape)
    cnt_ref[...] = jnp.broadcast_to(new_carry, cnt_ref.shape)

    ri_ref[...] = jnp.concatenate(idx + [r.astype(I32) for r in ranks], axis=0)
    rg_ref[...] = jnp.concatenate(gates + [jnp.zeros((TOP_K, tm), F32)], axis=0)


def _mix(x2, o_views, l_views, gb, z, conv_w, ga, gc, w_out_bf, gf, w_router, b_router, tm, batch, seq):
    t, d = x2.shape
    n = t // tm
    tps = seq // tm
    upper = jnp.asarray(np.triu(np.ones((tm, tm), np.float32), 1), dtype=BF16)
    sel = np.arange(LANES)[:, None] == 16 * (np.arange(ATTN_W)[None, :] // HEAD_DIM)
    expand = jnp.asarray(np.concatenate([sel, sel], axis=0), dtype=BF16)
    wr_pad = jnp.zeros((d, LANES), F32).at[:, :N_EXPERTS].set(w_router)
    wr2 = jnp.concatenate(_split_bf16(wr_pad), axis=1)
    br_pad = jnp.zeros((1, LANES), F32).at[0, :N_EXPERTS].set(b_router)

    rowd = pl.BlockSpec((tm, d), lambda i: (i, 0))
    rowa = pl.BlockSpec((tm, ATTN_W), lambda i: (i, 0))
    halo = tm // 16
    zprev = pl.BlockSpec((16, CONV_W), lambda i: (jnp.maximum(i * halo - 1, 0), 0))
    znext = pl.BlockSpec((16, CONV_W), lambda i: (jnp.minimum((i + 1) * halo, t // 16 - 1), 0))

    def view(width, dil):
        return pl.BlockSpec((1, tm // dil, dil * width), lambda i: (i // tps, i % tps, 0))

    def full(a):
        return pl.BlockSpec(a.shape, lambda i: (0,) * a.ndim)

    consts = (conv_w, ga, gc, w_out_bf, gf, wr2, br_pad, upper, expand)
    return pl.pallas_call(
        functools.partial(_mix_kernel, tm=tm, tiles_per_seq=tps),
        grid=(n,),
        in_specs=[rowd] + [view(ATTN_W, dil) for dil in DILATIONS] + [view(2 * LANES, dil) for dil in DILATIONS]
                 + [rowa, rowa, zprev, znext] + [full(a) for a in consts],
        out_specs=[rowd, pl.BlockSpec((tm * SUBLANES, LANES), lambda i: (i, 0)),
                   pl.BlockSpec((8, tm), lambda i: (0, i)), pl.BlockSpec((8, tm), lambda i: (0, i)),
                   pl.BlockSpec((N_EXPERTS, LANES), lambda i: (0, 0))],
        out_shape=[jax.ShapeDtypeStruct((t, d), F32), jax.ShapeDtypeStruct((t * SUBLANES, LANES), F32),
                   jax.ShapeDtypeStruct((8, t), I32), jax.ShapeDtypeStruct((8, t), F32),
                   jax.ShapeDtypeStruct((N_EXPERTS, LANES), F32)],
        scratch_shapes=[pltpu.VMEM((2, SLABS, tm, LANES), F32), pltpu.VMEM((2, 2, tm, LANES), F32),
                        pltpu.VMEM((N_EXPERTS, LANES), F32)],
        compiler_params=_cparams("arbitrary"),
        name="mix_router",
    )(x2, *o_views, *l_views, gb, z, z, z, *consts)


def _dispatch_kernel(zrow_ref, dest_ref, h_ref, xs_ref, slot_smem, zero_buf, sem_idx, sem_row, sem_zero,
                     *, tm, tb, n_tiles):
    i = pl.program_id(0)
    n = tm * TOP_K

    def idx_copy(tile, buf):
        return pltpu.make_async_copy(dest_ref.at[pl.ds(tile * n, n)],
                                     slot_smem.at[pl.ds(pl.multiple_of(buf * n, n), n)], sem_idx.at[buf])

    @pl.when(i == 0)
    def _():
        zero_buf[...] = jnp.zeros_like(zero_buf)
        for j in range(2 * N_EXPERTS):
            @pl.when(zrow_ref[j] >= 0)
            def _():
                start = pl.multiple_of(zrow_ref[j] * SUBLANES, tb * SUBLANES)
                pltpu.make_async_copy(zero_buf, xs_ref.at[pl.ds(start, tb * SUBLANES)], sem_zero).start()
        for j in range(2 * N_EXPERTS):
            @pl.when(zrow_ref[j] >= 0)
            def _():
                pltpu.make_async_copy(zero_buf, xs_ref.at[pl.ds(0, tb * SUBLANES)], sem_zero).wait()
        idx_copy(0, 0).start()

    cur = i % 2
    idx_copy(i, cur).wait()

    @pl.when(i + 1 < n_tiles)
    def _():
        idx_copy(i + 1, 1 - cur).start()

    bases = [cur * n + k * tm for k in range(TOP_K)]

    def body(t, c):
        src = h_ref.at[pl.ds(pl.multiple_of(t * SUBLANES, SUBLANES), SUBLANES)]
        for k in range(TOP_K):
            row = pl.multiple_of(slot_smem[bases[k] + t], SUBLANES)
            pltpu.make_async_copy(src, xs_ref.at[pl.ds(row, SUBLANES)], sem_row).start(priority=k % 2)
        return c

    lax.fori_loop(0, tm, body, 0, unroll=8)

    for _ in range(TOP_K):
        pltpu.make_async_copy(h_ref, xs_ref.at[pl.ds(0, tm * SUBLANES)], sem_row).wait()


def _dispatch(zrow, dest_tiles, h2_tiles, n_slots, tm, tb):
    t = h2_tiles.shape[0] // SUBLANES
    n_tiles = t // tm
    grid_spec = pltpu.PrefetchScalarGridSpec(
        num_scalar_prefetch=1,
        grid=(n_tiles,),
        in_specs=[pl.BlockSpec(memory_space=pl.ANY),
                  pl.BlockSpec((tm * SUBLANES, LANES), lambda i, zr: (i, 0))],
        out_specs=pl.BlockSpec(memory_space=pl.ANY),
        scratch_shapes=[pltpu.SMEM((2 * tm * TOP_K,), I32), pltpu.VMEM((tb * SUBLANES, LANES), F32),
                        pltpu.SemaphoreType.DMA((2,)), pltpu.SemaphoreType.DMA, pltpu.SemaphoreType.DMA],
    )
    return pl.pallas_call(
        functools.partial(_dispatch_kernel, tm=tm, tb=tb, n_tiles=n_tiles),
        grid_spec=grid_spec,
        out_shape=jax.ShapeDtypeStruct((n_slots * SUBLANES, LANES), F32),
        compiler_params=_cparams("arbitrary"),
        name="dispatch",
    )(zrow, dest_tiles, h2_tiles)


def _expert_kernel(be_ref, bx_ref, bv_ref, xs_ref, wgu_ref, bgu_ref, wd_ref, bd_ref, o_ref, wgu_bf, wd_bf,
                   *, tb, dff):
    i = pl.program_id(0)

    @pl.when(jnp.logical_or(i == 0, be_ref[i] != be_ref[jnp.maximum(i - 1, 0)]))
    def _():
        wgu_bf[...] = wgu_ref[0].astype(BF16)
        wd_bf[...] = wd_ref[0].astype(BF16)

    @pl.when(bv_ref[i] > 0)
    def _():
        x = _load_token_tiles(xs_ref, tb).astype(BF16)
        hu = jnp.dot(x, wgu_bf[...], preferred_element_type=F32) + bgu_ref[0]
        a = jnp.minimum(hu[:, :dff], SWIGLU_LIMIT)
        lin = jnp.clip(hu[:, dff:], -SWIGLU_LIMIT, SWIGLU_LIMIT)
        act = a * (1.0 / (1.0 + jnp.exp(-SWIGLU_ALPHA * a))) * (lin + 1.0)
        out = jnp.dot(act.astype(BF16), wd_bf[...], preferred_element_type=F32) + bd_ref[0]
        _store_token_tiles(o_ref, out, tb)

    @pl.when(bv_ref[i] == 0)
    def _():
        o_ref[...] = jnp.zeros_like(o_ref)


def _experts(blk_e, blk_x, blk_v, xs_tiles, wgu, bgu, wd, bd, tb):
    d, dff = wd.shape[2], wd.shape[1]
    n_slots = xs_tiles.shape[0] // SUBLANES
    nb = n_slots // tb
    rows = tb * SUBLANES
    grid_spec = pltpu.PrefetchScalarGridSpec(
        num_scalar_prefetch=3,
        grid=(nb,),
        in_specs=[pl.BlockSpec((rows, LANES), lambda i, be, bx, bv: (bx[i], 0)),
                  pl.BlockSpec((1, d, 2 * dff), lambda i, be, bx, bv: (be[i], 0, 0)),
                  pl.BlockSpec((1, 1, 2 * dff), lambda i, be, bx, bv: (be[i], 0, 0)),
                  pl.BlockSpec((1, dff, d), lambda i, be, bx, bv: (be[i], 0, 0)),
                  pl.BlockSpec((1, 1, d), lambda i, be, bx, bv: (be[i], 0, 0))],
        out_specs=pl.BlockSpec((rows, LANES), lambda i, be, bx, bv: (i, 0)),
        scratch_shapes=[pltpu.VMEM((d, 2 * dff), BF16), pltpu.VMEM((dff, d), BF16)],
    )
    return pl.pallas_call(
        functools.partial(_expert_kernel, tb=tb, dff=dff),
        grid_spec=grid_spec,
        out_shape=jax.ShapeDtypeStruct((n_slots * SUBLANES, LANES), F32),
        compiler_params=_cparams("arbitrary"),
        name="experts",
    )(blk_e, blk_x, blk_v, xs_tiles, wgu, bgu, wd, bd)


def _combine_kernel(dest_ref, x1_ref, rg_ref, fg_ref, mo_ref, out_ref, slot_smem, gbuf, sem_idx, sem_row,
                    *, tm, n_tiles):
    i = pl.program_id(0)
    n = tm * TOP_K

    def idx_copy(tile):
        buf = tile % 3
        return pltpu.make_async_copy(dest_ref.at[pl.ds(tile * n, n)],
                                     slot_smem.at[pl.ds(pl.multiple_of(buf * n, n), n)], sem_idx.at[buf])

    def issue(tile):
        gb = tile % 2
        bases = [(tile % 3) * n + k * tm for k in range(TOP_K)]
        dsts = [gbuf.at[gb, k] for k in range(TOP_K)]
        sem = sem_row.at[gb]

        def body(t, c):
            dst = pl.ds(pl.multiple_of(t * SUBLANES, SUBLANES), SUBLANES)
            for k in range(TOP_K):
                row = pl.multiple_of(slot_smem[bases[k] + t], SUBLANES)
                pltpu.make_async_copy(mo_ref.at[pl.ds(row, SUBLANES)], dsts[k].at[dst], sem).start(priority=k % 2)
            return c

        lax.fori_loop(0, tm, body, 0, unroll=8)

    @pl.when(i == 0)
    def _():
        idx_copy(0).start()
        if n_tiles > 1:
            idx_copy(1).start()
        idx_copy(0).wait()
        issue(0)

    @pl.when(i + 1 < n_tiles)
    def _():
        idx_copy(i + 1).wait()
        issue(i + 1)

    @pl.when(i + 2 < n_tiles)
    def _():
        idx_copy(i + 2).start()

    cur = i % 2
    for k in range(TOP_K):
        pltpu.make_async_copy(mo_ref.at[pl.ds(0, tm * SUBLANES)], gbuf.at[cur, k], sem_row.at[cur]).wait()

    g_rows = jnp.concatenate([rg_ref[...], jnp.zeros((LANES - 8, tm), F32)], axis=0)
    g = g_rows.T
    y = x1_ref[...]
    for k in range(TOP_K):
        y = y + g[:, k:k + 1] * _load_token_tiles(gbuf.at[cur, k], tm)
    out_ref[...] = _rms(y, fg_ref[...])


def _combine(dest_tiles, x1, rg, fg, mlp_out, tm):
    t, d = x1.shape
    n_tiles = t // tm
    rowd = pl.BlockSpec((tm, d), lambda i: (i, 0))
    return pl.pallas_call(
        functools.partial(_combine_kernel, tm=tm, n_tiles=n_tiles),
        grid=(n_tiles,),
        in_specs=[pl.BlockSpec(memory_space=pl.ANY), rowd,
                  pl.BlockSpec((8, tm), lambda i: (0, i)),
                  pl.BlockSpec((1, d), lambda i: (0, 0)),
                  pl.BlockSpec(memory_space=pl.ANY)],
        out_specs=rowd,
        out_shape=jax.ShapeDtypeStruct((t, d), F32),
        scratch_shapes=[pltpu.SMEM((3 * tm * TOP_K,), I32), pltpu.VMEM((2, TOP_K, tm * SUBLANES, LANES), F32),
                        pltpu.SemaphoreType.DMA((3,)), pltpu.SemaphoreType.DMA((2,))],
        compiler_params=_cparams("arbitrary"),
        name="combine",
    )(dest_tiles, x1, rg, fg, mlp_out)


def _block_tables(counts, idx, rank, tb, nb):
    padded = (counts + tb - 1) // tb * tb
    pad_end = jnp.cumsum(padded)
    pad_start = pad_end - padded
    sel = idx[..., None] == jnp.arange(N_EXPERTS, dtype=I32)
    dest = jnp.sum(jnp.where(sel, pad_start, 0), axis=-1) + rank
    blocks_e = padded // tb
    blk_end = jnp.cumsum(blocks_e)
    total = blk_end[-1]
    bid = jnp.arange(nb, dtype=I32)
    src = jnp.minimum(bid, total - 1)
    e = jnp.minimum(jnp.sum((src[:, None] >= blk_end[None, :]).astype(I32), axis=1), N_EXPERTS - 1)
    first = jnp.sum(jnp.where(e[:, None] == jnp.arange(N_EXPERTS), blk_end - blocks_e, 0), axis=1)
    cnt = jnp.sum(jnp.where(e[:, None] == jnp.arange(N_EXPERTS), counts, 0), axis=1)
    valid = jnp.where(bid < total, jnp.clip(cnt - (src - first) * tb, 0, tb), 0)
    tail = total + jnp.arange(N_EXPERTS, dtype=I32)
    zrow = jnp.concatenate([jnp.where(padded > 0, pad_end - tb, -1), jnp.where(tail < nb, tail * tb, -1)])
    return dest.astype(I32), e.astype(I32), src.astype(I32), valid.astype(I32), zrow.astype(I32)


def _tile_major(dest, tm):
    k, t = dest.shape
    return (dest * SUBLANES).reshape(k, t // tm, tm).transpose(1, 0, 2).reshape(-1)


def kernel(x, mix_norm_g, w_in, conv_w, attn_out_norm_g, conv_out_norm_g, w_out, ffn_norm_g, w_router,
           b_router, w_gate_up, b_gate_up, w_down, b_down, final_norm_g):
    batch, seq, d = x.shape
    t = batch * seq
    assert w_in.shape[0] == 1, "single-layer trunk: the final RMSNorm is fused into the combine kernel"
    assert d == SUBLANES * LANES, "one token row must fill exactly one (8, 128) f32 tile"
    tm = 512
    tb = 512
    tc = 256
    nb = (t * TOP_K) // tb + N_EXPERTS
    x2 = x.reshape(t, d)

    *qkv, gb, z = _proj(x2, mix_norm_g[0][None, :], w_in[0].astype(BF16), tm, batch, seq)
    o_views, l_views = [], []
    for di, dil in enumerate(DILATIONS):
        o, lse = _attn_branch(qkv[di], qkv[3 + di], qkv[6 + di], dil)
        o_views.append(o)
        l_views.append(lse)

    x1, h2, ri, rg, cnt = _mix(x2, o_views, l_views, gb, z, conv_w[0], attn_out_norm_g[0][None, :],
                               conv_out_norm_g[0][None, :], w_out[0].astype(BF16), ffn_norm_g[0][None, :],
                               w_router[0], b_router[0], tm, batch, seq)

    counts = cnt[:, 0].astype(I32)
    dest, blk_e, blk_x, blk_v, zrow = _block_tables(counts, ri[:TOP_K], ri[TOP_K:], tb, nb)

    xs = _dispatch(zrow, _tile_major(dest, tm), h2, nb * tb, tm, tb)
    mo = _experts(blk_e, blk_x, blk_v, xs, w_gate_up[0], b_gate_up[0][:, None, :],
                  w_down[0], b_down[0][:, None, :], tb)
    out = _combine(_tile_major(dest, tc), x1, rg, final_norm_g[None, :], mo, tc)
    return out.reshape(batch, seq, d)
```

```python
import functools

import numpy as np
import jax
import jax.numpy as jnp
from jax import lax
from jax.experimental import pallas as pl
from jax.experimental.pallas import tpu as pltpu

F32 = jnp.float32
BF16 = jnp.bfloat16
I32 = jnp.int32

HEAD_DIM = 64
N_HEADS = 8
ATTN_W = N_HEADS * HEAD_DIM
CONV_W = 512
N_EXPERTS = 32
TOP_K = 4
DILATIONS = (1, 4, 16)
SIDE = 64
SWIGLU_ALPHA = 1.702
SWIGLU_LIMIT = 7.0
RMS_EPS = 1e-5
NEG = -1e30
LOG2E = 1.4426950408889634

LANES = 128
SLABS = ATTN_W // LANES
Q_SUB = 128
K_WIN = Q_SUB + 2 * SIDE
VMEM_LIMIT = 56 * 1024 * 1024


def _cparams(*sem):
    return pltpu.CompilerParams(dimension_semantics=sem, vmem_limit_bytes=VMEM_LIMIT)


def _rms(x, g):
    return x * lax.rsqrt(jnp.mean(x * x, axis=-1, keepdims=True) + RMS_EPS) * g


SUBLANES = 8
ISSUE_GROUP = 8


def _store_token_tiles(ref, x, n):
    for s in range(x.shape[1] // LANES):
        ref[pl.ds(s, n, stride=SUBLANES), :] = x[:, s * LANES:(s + 1) * LANES]


def _load_token_tiles(ref, n):
    return jnp.concatenate([ref[pl.ds(s, n, stride=SUBLANES), :] for s in range(SUBLANES)], axis=1)


def _split_bf16(a):
    hi = a.astype(BF16)
    return hi, (a - hi.astype(F32)).astype(BF16)


def _proj_kernel(x_ref, g_ref, w_ref, *refs, tm):
    outs = refs[:9]
    gb_ref, z_ref, pbuf = refs[9:]
    h = _rms(x_ref[...], g_ref[...]).astype(BF16)

    def proj(j):
        return jnp.dot(h, w_ref[:, j * ATTN_W:(j + 1) * ATTN_W], preferred_element_type=F32)

    for j in range(3):
        p = proj(j)
        if j == 0:
            p = p * (LOG2E * HEAD_DIM ** -0.5)
        outs[3 * j][0] = p.astype(BF16)
        for c in range(SLABS):
            pbuf[j, c] = p[:, c * LANES:(c + 1) * LANES]
        for di in (1, 2):
            dil = DILATIONS[di]
            o = outs[3 * j + di]
            for r in range(dil):
                for c in range(SLABS):
                    lo = r * ATTN_W + c * LANES
                    o[0, :, lo:lo + LANES] = pbuf[j, c, pl.ds(r, tm // dil, stride=dil), :].astype(BF16)
    gb_ref[...] = proj(3).astype(BF16)
    z_ref[...] = (proj(4) * proj(5)).astype(BF16)


def _proj(x2, g, w_in_bf, tm, batch, seq):
    t, d = x2.shape
    tps = seq // tm
    row = pl.BlockSpec((tm, ATTN_W), lambda i: (i, 0))
    view_specs, view_shapes = [], []
    for _ in range(3):
        for dil in DILATIONS:
            view_specs.append(pl.BlockSpec((1, tm // dil, dil * ATTN_W), lambda i: (i // tps, i % tps, 0)))
            view_shapes.append(jax.ShapeDtypeStruct((batch, seq // dil, dil * ATTN_W), BF16))
    flat = jax.ShapeDtypeStruct((t, ATTN_W), BF16)
    return pl.pallas_call(
        functools.partial(_proj_kernel, tm=tm),
        grid=(t // tm,),
        in_specs=[pl.BlockSpec((tm, d), lambda i: (i, 0)),
                  pl.BlockSpec((1, d), lambda i: (0, 0)),
                  pl.BlockSpec(w_in_bf.shape, lambda i: (0, 0))],
        out_specs=view_specs + [row, row],
        out_shape=view_shapes + [flat, flat],
        scratch_shapes=[pltpu.VMEM((3, SLABS, tm, LANES), F32)],
        compiler_params=_cparams("parallel"),
        name="proj",
    )(x2, g, w_in_bf)


def _bias_tables(dil):
    row = np.arange(Q_SUB)[:, None]
    col = np.arange(K_WIN)[None, :]
    rel = col - SIDE - row
    band = np.abs(rel) <= SIDE
    slopes = 2.0 ** (-8.0 * np.arange(1, N_HEADS + 1) / N_HEADS)
    tabs = []
    for t in range(4):
        ok = band.copy()
        if t & 1:
            ok &= col >= SIDE
        if t & 2:
            ok &= col < SIDE + Q_SUB
        for h in range(N_HEADS):
            tabs.append(np.where(ok, -slopes[h] * np.abs(rel) * dil * LOG2E, NEG))
    return jnp.asarray(np.stack(tabs), dtype=F32)


def _attn_kernel(q_ref, kp_ref, km_ref, kn_ref, vp_ref, vm_ref, vn_ref, bias_ref,
                 o_ref, st_ref, kbuf, vbuf, *, tq, n_sub_total):
    i = pl.program_id(2)
    pieces = ((0, SIDE, kp_ref, vp_ref), (SIDE, SIDE + tq, km_ref, vm_ref), (SIDE + tq, 2 * SIDE + tq, kn_ref, vn_ref))
    for lo_row, hi_row, k_ref, v_ref in pieces:
        kbuf[lo_row:hi_row] = k_ref[0]
        for hp in range(N_HEADS // 2):
            vbuf[lo_row:hi_row, 2 * hp * LANES:(2 * hp + 1) * LANES] = v_ref[0, :, hp * LANES:(hp + 1) * LANES]

    @pl.when((pl.program_id(0) == 0) & (pl.program_id(1) == 0) & (i == 0))
    def _():
        for hp in range(N_HEADS // 2):
            vbuf[:, (2 * hp + 1) * LANES:(2 * hp + 2) * LANES] = jnp.ones((tq + 2 * SIDE, LANES), BF16)

    lane = lax.broadcasted_iota(I32, (Q_SUB, LANES), 1)
    lo = lane < HEAD_DIM
    grp = lane // 16
    n_sub = tq // Q_SUB

    def sub(j, carry):
        r0 = pl.multiple_of(j * Q_SUB, Q_SUB)
        g_sub = i * n_sub + j
        tbl = (g_sub == 0).astype(I32) + 2 * (g_sub == n_sub_total - 1).astype(I32)
        m_tile = jnp.zeros((Q_SUB, LANES), F32)
        l_tile = jnp.zeros((Q_SUB, LANES), F32)
        for hp in range(N_HEADS // 2):
            cols = slice(hp * LANES, (hp + 1) * LANES)
            qp = q_ref[0, pl.ds(r0, Q_SUB), cols]
            kw = kbuf[pl.ds(r0, K_WIN), cols]
            vw = vbuf[pl.ds(r0, K_WIN), 2 * hp * LANES:(2 * hp + 2) * LANES]
            outs = []
            for par in range(2):
                qm = jnp.where(lo if par == 0 else jnp.logical_not(lo), qp, jnp.zeros_like(qp))
                s = lax.dot_general(qm, kw, (((1,), (1,)), ((), ())), preferred_element_type=F32)
                s = s + bias_ref[tbl * N_HEADS + 2 * hp + par]
                m = jnp.max(s, axis=1, keepdims=True)
                p = jnp.exp2(s - m).astype(BF16)
                pvl = jnp.dot(p, vw, preferred_element_type=F32)
                outs.append(pvl[:, :LANES])
                m_tile = jnp.where(grp == 2 * hp + par, m, m_tile)
                l_tile = jnp.where(grp == 2 * hp + par, pvl[:, LANES:], l_tile)
            o_ref[0, pl.ds(r0, Q_SUB), cols] = jnp.where(lo, outs[0], outs[1]).astype(BF16)
        st_ref[0, pl.ds(r0, Q_SUB), 0:LANES] = m_tile
        st_ref[0, pl.ds(r0, Q_SUB), LANES:2 * LANES] = l_tile
        return carry

    lax.fori_loop(0, n_sub, sub, 0, unroll=min(n_sub, 4))


def _attn_branch(qv, kv, vv, dil):
    batch, sd, _ = qv.shape
    tq = min(1024, sd)
    n_tiles = sd // tq
    bias = _bias_tables(dil)
    r_side = tq // SIDE
    last_side = sd // SIDE - 1

    main = pl.BlockSpec((1, tq, ATTN_W), lambda b, r, i: (b, i, r))
    prev = pl.BlockSpec((1, SIDE, ATTN_W), lambda b, r, i: (b, jnp.maximum(i * r_side - 1, 0), r))
    nxt = pl.BlockSpec((1, SIDE, ATTN_W), lambda b, r, i: (b, jnp.minimum((i + 1) * r_side, last_side), r))
    st_spec = pl.BlockSpec((1, tq, 2 * LANES), lambda b, r, i: (b, i, r))
    return pl.pallas_call(
        functools.partial(_attn_kernel, tq=tq, n_sub_total=sd // Q_SUB),
        grid=(batch, dil, n_tiles),
        in_specs=[main, prev, main, nxt, prev, main, nxt,
                  pl.BlockSpec(bias.shape, lambda b, r, i: (0, 0, 0))],
        out_specs=[main, st_spec],
        out_shape=[jax.ShapeDtypeStruct((batch, sd, dil * ATTN_W), BF16),
                   jax.ShapeDtypeStruct((batch, sd, dil * 2 * LANES), F32)],
        scratch_shapes=[pltpu.VMEM((tq + 2 * SIDE, ATTN_W), BF16),
                        pltpu.VMEM((tq + 2 * SIDE, 2 * ATTN_W), BF16)],
        compiler_params=_cparams("arbitrary", "arbitrary", "arbitrary"),
        name=f"attn_d{dil}",
    )(qv, kv, kv, kv, vv, vv, vv, bias)


def _mix_kernel(x_ref, o1_ref, o4_ref, o16_ref, l1_ref, l4_ref, l16_ref, gb_ref, z_ref, zp_ref, zn_ref,
                cw_ref, ga_ref, gc_ref, wo_ref, gf_ref, wr2_ref, br_ref, upper_ref, expand_ref,
                x1_ref, h2_ref, ri_ref, rg_ref, cnt_ref, onat, lnat, carry, *, tm, tiles_per_seq):
    i = pl.program_id(0)

    @pl.when(i == 0)
    def _():
        carry[...] = jnp.zeros_like(carry)

    for bi, (o_ref, l_ref, dil) in enumerate(((o4_ref, l4_ref, 4), (o16_ref, l16_ref, 16))):
        n = tm // dil
        for r in range(dil):
            for c in range(2):
                lo = (2 * r + c) * LANES
                lnat[bi, c, pl.ds(r, n, stride=dil), :] = l_ref[0, :, lo:lo + LANES]
            for c in range(SLABS):
                lo = r * ATTN_W + c * LANES
                onat[bi, c, pl.ds(r, n, stride=dil), :] = o_ref[0, :, lo:lo + LANES].astype(F32)

    ms = [l1_ref[0, :, 0:LANES], lnat[0, 0], lnat[1, 0]]
    ls = [l1_ref[0, :, LANES:2 * LANES], lnat[0, 1], lnat[1, 1]]
    mx = jnp.maximum(jnp.maximum(ms[0], ms[1]), ms[2])
    w = [jnp.exp2(t - mx) for t in ms]
    inv = 1.0 / (w[0] * ls[0] + w[1] * ls[1] + w[2] * ls[2])
    wide = []
    for g in range(3):
        hi_lo = jnp.concatenate(_split_bf16(w[g] * inv), axis=1)
        wide.append(jnp.dot(hi_lo, expand_ref[...], preferred_element_type=F32))
    slabs = []
    for c in range(SLABS):
        cs = slice(c * LANES, (c + 1) * LANES)
        slabs.append(wide[0][:, cs] * o1_ref[0, :, cs].astype(F32)
                     + wide[1][:, cs] * onat[0, c] + wide[2][:, cs] * onat[1, c])
    attn = jnp.concatenate(slabs, axis=1)

    z = z_ref[...].astype(F32)
    row = lax.broadcasted_iota(I32, (tm, 1), 0)
    seq_first = (i % tiles_per_seq) == 0
    seq_last = (i % tiles_per_seq) == tiles_per_seq - 1
    z_before = jnp.where(seq_first, 0.0, zp_ref[15:16, :].astype(F32))
    z_after = jnp.where(seq_last, 0.0, zn_ref[0:1, :].astype(F32))
    z_up = jnp.where(row == 0, z_before, pltpu.roll(z, 1, 0))
    z_dn = jnp.where(row == tm - 1, z_after, pltpu.roll(z, tm - 1, 0))
    conv = gb_ref[...].astype(F32) * (cw_ref[0:1, :] * z_up + cw_ref[1:2, :] * z + cw_ref[2:3, :] * z_dn)

    na = _rms(attn, ga_ref[...]).astype(BF16)
    nc = _rms(conv, gc_ref[...]).astype(BF16)
    mix = (jnp.dot(na, wo_ref[0:ATTN_W, :], preferred_element_type=F32)
           + jnp.dot(nc, wo_ref[ATTN_W:, :], preferred_element_type=F32))
    x1 = x_ref[...] + mix
    x1_ref[...] = x1
    h2 = _rms(x1, gf_ref[...])
    _store_token_tiles(h2_ref, h2, tm)

    hh, hl = _split_bf16(h2)
    both = jnp.dot(hh, wr2_ref[...], preferred_element_type=F32)
    logits = (both[:, :LANES] + both[:, LANES:]
              + jnp.dot(hl, wr2_ref[:, :LANES], preferred_element_type=F32) + br_ref[...])
    lt = logits.T[0:N_EXPERTS, :]

    erow = lax.broadcasted_iota(I32, (N_EXPERTS, tm), 0)
    work = lt
    idx, val = [], []
    for _ in range(TOP_K):
        mk = jnp.max(work, axis=0, keepdims=True)
        ik = jnp.min(jnp.where(work == mk, erow, N_EXPERTS), axis=0, keepdims=True)
        idx.append(ik)
        val.append(mk)
        work = jnp.where(erow == ik, 2 * NEG, work)
    ex = [jnp.exp(v - val[0]) for v in val]
    ginv = 1.0 / (ex[0] + ex[1] + ex[2] + ex[3])
    gates = [e * ginv for e in ex]

    hit = [erow == ik for ik in idx]
    onehot = (hit[0] | hit[1] | hit[2] | hit[3]).astype(F32)
    before = jnp.dot(onehot.astype(BF16), upper_ref[...], preferred_element_type=F32) + carry[:, 0:1]
    ranks = [jnp.sum(jnp.where(h, before, 0.0), axis=0, keepdims=True) for h in hit]
    new_carry = carry[:, 0:1] + jnp.sum(onehot, axis=1, keepdims=True)
    carry[...] = jnp.broadcast_to(new_carry, carry.shape)
    cnt_ref[...] = jnp.broadcast_to(new_carry, cnt_ref.shape)

    ri_ref[...] = jnp.concatenate(idx + [r.astype(I32) for r in ranks], axis=0)
    rg_ref[...] = jnp.concatenate(gates + [jnp.zeros((TOP_K, tm), F32)], axis=0)


def _mix(x2, o_views, l_views, gb, z, conv_w, ga, gc, w_out_bf, gf, w_router, b_router, tm, batch, seq):
    t, d = x2.shape
    n = t // tm
    tps = seq // tm
    upper = jnp.asarray(np.triu(np.ones((tm, tm), np.float32), 1), dtype=BF16)
    sel = np.arange(LANES)[:, None] == 16 * (np.arange(ATTN_W)[None, :] // HEAD_DIM)
    expand = jnp.asarray(np.concatenate([sel, sel], axis=0), dtype=BF16)
    wr_pad = jnp.zeros((d, LANES), F32).at[:, :N_EXPERTS].set(w_router)
    wr2 = jnp.concatenate(_split_bf16(wr_pad), axis=1)
    br_pad = jnp.zeros((1, LANES), F32).at[0, :N_EXPERTS].set(b_router)

    rowd = pl.BlockSpec((tm, d), lambda i: (i, 0))
    rowa = pl.BlockSpec((tm, ATTN_W), lambda i: (i, 0))
    halo = tm // 16
    zprev = pl.BlockSpec((16, CONV_W), lambda i: (jnp.maximum(i * halo - 1, 0), 0))
    znext = pl.BlockSpec((16, CONV_W), lambda i: (jnp.minimum((i + 1) * halo, t // 16 - 1), 0))

    def view(width, dil):
        return pl.BlockSpec((1, tm // dil, dil * width), lambda i: (i // tps, i % tps, 0))

    def full(a):
        return pl.BlockSpec(a.shape, lambda i: (0,) * a.ndim)

    consts = (conv_w, ga, gc, w_out_bf, gf, wr2, br_pad, upper, expand)
    return pl.pallas_call(
        functools.partial(_mix_kernel, tm=tm, tiles_per_seq=tps),
        grid=(n,),
        in_specs=[rowd] + [view(ATTN_W, dil) for dil in DILATIONS] + [view(2 * LANES, dil) for dil in DILATIONS]
                 + [rowa, rowa, zprev, znext] + [full(a) for a in consts],
        out_specs=[rowd, pl.BlockSpec((tm * SUBLANES, LANES), lambda i: (i, 0)),
                   pl.BlockSpec((8, tm), lambda i: (0, i)), pl.BlockSpec((8, tm), lambda i: (0, i)),
                   pl.BlockSpec((N_EXPERTS, LANES), lambda i: (0, 0))],
        out_shape=[jax.ShapeDtypeStruct((t, d), F32), jax.ShapeDtypeStruct((t * SUBLANES, LANES), F32),
                   jax.ShapeDtypeStruct((8, t), I32), jax.ShapeDtypeStruct((8, t), F32),
                   jax.ShapeDtypeStruct((N_EXPERTS, LANES), F32)],
        scratch_shapes=[pltpu.VMEM((2, SLABS, tm, LANES), F32), pltpu.VMEM((2, 2, tm, LANES), F32),
                        pltpu.VMEM((N_EXPERTS, LANES), F32)],
        compiler_params=_cparams("arbitrary"),
        name="mix_router",
    )(x2, *o_views, *l_views, gb, z, z, z, *consts)


def _dispatch_kernel(zrow_ref, dest_ref, h_ref, xs_ref, slot_smem, zero_buf, sem_idx, sem_row, sem_zero,
                     *, tm, tb, n_tiles):
    i = pl.program_id(0)
    n = tm * TOP_K

    def idx_copy(tile, buf):
        return pltpu.make_async_copy(dest_ref.at[pl.ds(tile * n, n)],
                                     slot_smem.at[pl.ds(pl.multiple_of(buf * n, n), n)], sem_idx.at[buf])

    @pl.when(i == 0)
    def _():
        zero_buf[...] = jnp.zeros_like(zero_buf)
        for j in range(2 * N_EXPERTS):
            @pl.when(zrow_ref[j] >= 0)
            def _():
                start = pl.multiple_of(zrow_ref[j] * SUBLANES, tb * SUBLANES)
                pltpu.make_async_copy(zero_buf, xs_ref.at[pl.ds(start, tb * SUBLANES)], sem_zero).start()
        for j in range(2 * N_EXPERTS):
            @pl.when(zrow_ref[j] >= 0)
            def _():
                pltpu.make_async_copy(zero_buf, xs_ref.at[pl.ds(0, tb * SUBLANES)], sem_zero).wait()
        idx_copy(0, 0).start()

    cur = i % 2
    idx_copy(i, cur).wait()

    @pl.when(i + 1 < n_tiles)
    def _():
        idx_copy(i + 1, 1 - cur).start()

    bases = [cur * n + k * tm for k in range(TOP_K)]

    def body(t, c):
        src = h_ref.at[pl.ds(pl.multiple_of(t * SUBLANES, SUBLANES), SUBLANES)]
        for k in range(TOP_K):
            row = pl.multiple_of(slot_smem[bases[k] + t], SUBLANES)
            pltpu.make_async_copy(src, xs_ref.at[pl.ds(row, SUBLANES)], sem_row).start(priority=k % 2)
        return c

    lax.fori_loop(0, tm, body, 0, unroll=ISSUE_GROUP)

    for _ in range(TOP_K):
        pltpu.make_async_copy(h_ref, xs_ref.at[pl.ds(0, tm * SUBLANES)], sem_row).wait()


def _dispatch(zrow, dest_tiles, h2_tiles, n_slots, tm, tb):
    t = h2_tiles.shape[0] // SUBLANES
    n_tiles = t // tm
    grid_spec = pltpu.PrefetchScalarGridSpec(
        num_scalar_prefetch=1,
        grid=(n_tiles,),
        in_specs=[pl.BlockSpec(memory_space=pl.ANY),
                  pl.BlockSpec((tm * SUBLANES, LANES), lambda i, zr: (i, 0))],
        out_specs=pl.BlockSpec(memory_space=pl.ANY),
        scratch_shapes=[pltpu.SMEM((2 * tm * TOP_K,), I32), pltpu.VMEM((tb * SUBLANES, LANES), F32),
                        pltpu.SemaphoreType.DMA((2,)), pltpu.SemaphoreType.DMA, pltpu.SemaphoreType.DMA],
    )
    return pl.pallas_call(
        functools.partial(_dispatch_kernel, tm=tm, tb=tb, n_tiles=n_tiles),
        grid_spec=grid_spec,
        out_shape=jax.ShapeDtypeStruct((n_slots * SUBLANES, LANES), F32),
        compiler_params=_cparams("arbitrary"),
        name="dispatch",
    )(zrow, dest_tiles, h2_tiles)


def _expert_kernel(be_ref, bx_ref, bv_ref, xs_ref, wgu_ref, bgu_ref, wd_ref, bd_ref, o_ref, wgu_bf, wd_bf,
                   *, tb, dff):
    i = pl.program_id(0)

    @pl.when(jnp.logical_or(i == 0, be_ref[i] != be_ref[jnp.maximum(i - 1, 0)]))
    def _():
        wgu_bf[...] = wgu_ref[0].astype(BF16)
        wd_bf[...] = wd_ref[0].astype(BF16)

    @pl.when(bv_ref[i] > 0)
    def _():
        x = _load_token_tiles(xs_ref, tb).astype(BF16)
        hu = jnp.dot(x, wgu_bf[...], preferred_element_type=F32) + bgu_ref[0]
        a = jnp.minimum(hu[:, :dff], SWIGLU_LIMIT)
        lin = jnp.clip(hu[:, dff:], -SWIGLU_LIMIT, SWIGLU_LIMIT)
        act = a * (1.0 / (1.0 + jnp.exp(-SWIGLU_ALPHA * a))) * (lin + 1.0)
        out = jnp.dot(act.astype(BF16), wd_bf[...], preferred_element_type=F32) + bd_ref[0]
        _store_token_tiles(o_ref, out, tb)

    @pl.when(bv_ref[i] == 0)
    def _():
        o_ref[...] = jnp.zeros_like(o_ref)


def _experts(blk_e, blk_x, blk_v, xs_tiles, wgu, bgu, wd, bd, tb):
    d, dff = wd.shape[2], wd.shape[1]
    n_slots = xs_tiles.shape[0] // SUBLANES
    nb = n_slots // tb
    rows = tb * SUBLANES
    grid_spec = pltpu.PrefetchScalarGridSpec(
        num_scalar_prefetch=3,
        grid=(nb,),
        in_specs=[pl.BlockSpec((rows, LANES), lambda i, be, bx, bv: (bx[i], 0)),
                  pl.BlockSpec((1, d, 2 * dff), lambda i, be, bx, bv: (be[i], 0, 0)),
                  pl.BlockSpec((1, 1, 2 * dff), lambda i, be, bx, bv: (be[i], 0, 0)),
                  pl.BlockSpec((1, dff, d), lambda i, be, bx, bv: (be[i], 0, 0)),
                  pl.BlockSpec((1, 1, d), lambda i, be, bx, bv: (be[i], 0, 0))],
        out_specs=pl.BlockSpec((rows, LANES), lambda i, be, bx, bv: (i, 0)),
        scratch_shapes=[pltpu.VMEM((d, 2 * dff), BF16), pltpu.VMEM((dff, d), BF16)],
    )
    return pl.pallas_call(
        functools.partial(_expert_kernel, tb=tb, dff=dff),
        grid_spec=grid_spec,
        out_shape=jax.ShapeDtypeStruct((n_slots * SUBLANES, LANES), F32),
        compiler_params=_cparams("arbitrary"),
        name="experts",
    )(blk_e, blk_x, blk_v, xs_tiles, wgu, bgu, wd, bd)


def _combine_kernel(dest_ref, x1_ref, rg_ref, fg_ref, mo_ref, out_ref, slot_smem, gbuf, sem_idx, sem_row,
                    *, tm, n_tiles):
    i = pl.program_id(0)
    n = tm * TOP_K

    def idx_copy(tile):
        buf = tile % 3
        return pltpu.make_async_copy(dest_ref.at[pl.ds(tile * n, n)],
                                     slot_smem.at[pl.ds(pl.multiple_of(buf * n, n), n)], sem_idx.at[buf])

    def issue(tile):
        gb = tile % 2
        bases = [(tile % 3) * n + k * tm for k in range(TOP_K)]
        dsts = [gbuf.at[gb, k] for k in range(TOP_K)]
        sem = sem_row.at[gb]

        def body(t, c):
            dst = pl.ds(pl.multiple_of(t * SUBLANES, SUBLANES), SUBLANES)
            for k in range(TOP_K):
                row = pl.multiple_of(slot_smem[bases[k] + t], SUBLANES)
                pltpu.make_async_copy(mo_ref.at[pl.ds(row, SUBLANES)], dsts[k].at[dst], sem).start(priority=k % 2)
            return c

        lax.fori_loop(0, tm, body, 0, unroll=ISSUE_GROUP)

    @pl.when(i == 0)
    def _():
        idx_copy(0).start()
        if n_tiles > 1:
            idx_copy(1).start()
        idx_copy(0).wait()
        issue(0)

    @pl.when(i + 1 < n_tiles)
    def _():
        idx_copy(i + 1).wait()
        issue(i + 1)

    @pl.when(i + 2 < n_tiles)
    def _():
        idx_copy(i + 2).start()

    cur = i % 2
    for k in range(TOP_K):
        pltpu.make_async_copy(mo_ref.at[pl.ds(0, tm * SUBLANES)], gbuf.at[cur, k], sem_row.at[cur]).wait()

    g_rows = jnp.concatenate([rg_ref[...], jnp.zeros((LANES - 8, tm), F32)], axis=0)
    g = g_rows.T
    y = x1_ref[...]
    for k in range(TOP_K):
        y = y + g[:, k:k + 1] * _load_token_tiles(gbuf.at[cur, k], tm)
    out_ref[...] = _rms(y, fg_ref[...])


def _combine(dest_tiles, x1, rg, fg, mlp_out, tm):
    t, d = x1.shape
    n_tiles = t // tm
    rowd = pl.BlockSpec((tm, d), lambda i: (i, 0))
    return pl.pallas_call(
        functools.partial(_combine_kernel, tm=tm, n_tiles=n_tiles),
        grid=(n_tiles,),
        in_specs=[pl.BlockSpec(memory_space=pl.ANY), rowd,
                  pl.BlockSpec((8, tm), lambda i: (0, i)),
                  pl.BlockSpec((1, d), lambda i: (0, 0)),
                  pl.BlockSpec(memory_space=pl.ANY)],
        out_specs=rowd,
        out_shape=jax.ShapeDtypeStruct((t, d), F32),
        scratch_shapes=[pltpu.SMEM((3 * tm * TOP_K,), I32), pltpu.VMEM((2, TOP_K, tm * SUBLANES, LANES), F32),
                        pltpu.SemaphoreType.DMA((3,)), pltpu.SemaphoreType.DMA((2,))],
        compiler_params=_cparams("arbitrary"),
        name="combine",
    )(dest_tiles, x1, rg, fg, mlp_out)


def _block_tables(counts, idx, rank, tb, nb):
    padded = (counts + tb - 1) // tb * tb
    pad_end = jnp.cumsum(padded)
    pad_start = pad_end - padded
    sel = idx[..., None] == jnp.arange(N_EXPERTS, dtype=I32)
    dest = jnp.sum(jnp.where(sel, pad_start, 0), axis=-1) + rank
    blocks_e = padded // tb
    blk_end = jnp.cumsum(blocks_e)
    total = blk_end[-1]
    bid = jnp.arange(nb, dtype=I32)
    src = jnp.minimum(bid, total - 1)
    e = jnp.minimum(jnp.sum((src[:, None] >= blk_end[None, :]).astype(I32), axis=1), N_EXPERTS - 1)
    first = jnp.sum(jnp.where(e[:, None] == jnp.arange(N_EXPERTS), blk_end - blocks_e, 0), axis=1)
    cnt = jnp.sum(jnp.where(e[:, None] == jnp.arange(N_EXPERTS), counts, 0), axis=1)
    valid = jnp.where(bid < total, jnp.clip(cnt - (src - first) * tb, 0, tb), 0)
    tail = total + jnp.arange(N_EXPERTS, dtype=I32)
    zrow = jnp.concatenate([jnp.where(padded > 0, pad_end - tb, -1), jnp.where(tail < nb, tail * tb, -1)])
    return dest.astype(I32), e.astype(I32), src.astype(I32), valid.astype(I32), zrow.astype(I32)


def _tile_major(dest, tm):
    k, t = dest.shape
    return (dest * SUBLANES).reshape(k, t // tm, tm).transpose(1, 0, 2).reshape(-1)


def kernel(x, mix_norm_g, w_in, conv_w, attn_out_norm_g, conv_out_norm_g, w_out, ffn_norm_g, w_router,
           b_router, w_gate_up, b_gate_up, w_down, b_down, final_norm_g):
    batch, seq, d = x.shape
    t = batch * seq
    assert w_in.shape[0] == 1, "single-layer trunk: the final RMSNorm is fused into the combine kernel"
    assert d == SUBLANES * LANES, "one token row must fill exactly one (8, 128) f32 tile"
    tm = 512
    tb = 512
    tc = 512
    td = 1024
    nb = (t * TOP_K) // tb + N_EXPERTS
    x2 = x.reshape(t, d)

    *qkv, gb, z = _proj(x2, mix_norm_g[0][None, :], w_in[0].astype(BF16), tm, batch, seq)
    o_views, l_views = [], []
    for di, dil in enumerate(DILATIONS):
        o, lse = _attn_branch(qkv[di], qkv[3 + di], qkv[6 + di], dil)
        o_views.append(o)
        l_views.append(lse)

    x1, h2, ri, rg, cnt = _mix(x2, o_views, l_views, gb, z, conv_w[0], attn_out_norm_g[0][None, :],
                               conv_out_norm_g[0][None, :], w_out[0].astype(BF16), ffn_norm_g[0][None, :],
                               w_router[0], b_router[0], tm, batch, seq)

    counts = cnt[:, 0].astype(I32)
    dest, blk_e, blk_x, blk_v, zrow = _block_tables(counts, ri[:TOP_K], ri[TOP_K:], tb, nb)

    xs = _dispatch(zrow, _tile_major(dest, td), h2, nb * tb, td, tb)
    mo = _experts(blk_e, blk_x, blk_v, xs, w_gate_up[0], b_gate_up[0][:, None, :],
                  w_down[0], b_down[0][:, None, :], tb)
    out = _combine(_tile_major(dest, tc), x1, rg, final_norm_g[None, :], mo, tc)
    return out.reshape(batch, seq, d)
```

```python
import functools

import numpy as np
import jax
import jax.numpy as jnp
from jax import lax
from jax.experimental import pallas as pl
from jax.experimental.pallas import tpu as pltpu

F32 = jnp.float32
BF16 = jnp.bfloat16
I32 = jnp.int32

HEAD_DIM = 64
N_HEADS = 8
ATTN_W = N_HEADS * HEAD_DIM
CONV_W = 512
N_EXPERTS = 32
TOP_K = 4
DILATIONS = (1, 4, 16)
SIDE = 64
SWIGLU_ALPHA = 1.702
SWIGLU_LIMIT = 7.0
RMS_EPS = 1e-5
NEG = -1e30
LOG2E = 1.4426950408889634

LANES = 128
SLABS = ATTN_W // LANES
Q_SUB = 128
K_WIN = Q_SUB + 2 * SIDE
VMEM_LIMIT = 56 * 1024 * 1024


def _cparams(*sem):
    return pltpu.CompilerParams(dimension_semantics=sem, vmem_limit_bytes=VMEM_LIMIT)


def _rms(x, g):
    return x * lax.rsqrt(jnp.mean(x * x, axis=-1, keepdims=True) + RMS_EPS) * g


SUBLANES = 8
BF16_ROWS = 16
STAT_LANES = LANES // N_HEADS
ROUTE_ROWS = 2 * TOP_K
ISSUE_GROUP = 8
GATHER_RING = 2
IDX_RING = GATHER_RING + 1


def _store_token_tiles(ref, x, n):
    for s in range(x.shape[1] // LANES):
        ref[pl.ds(s, n, stride=SUBLANES), :] = x[:, s * LANES:(s + 1) * LANES]


def _load_token_tiles(ref, n):
    return jnp.concatenate([ref[pl.ds(s, n, stride=SUBLANES), :] for s in range(SUBLANES)], axis=1)


def _split_bf16(a):
    hi = a.astype(BF16)
    return hi, (a - hi.astype(F32)).astype(BF16)


def _proj_kernel(x_ref, g_ref, w_ref, *refs, tm):
    outs = refs[:9]
    gb_ref, z_ref, pbuf = refs[9:]
    h = _rms(x_ref[...], g_ref[...]).astype(BF16)

    def proj(j):
        return jnp.dot(h, w_ref[:, j * ATTN_W:(j + 1) * ATTN_W], preferred_element_type=F32)

    for j in range(3):
        p = proj(j)
        if j == 0:
            p = p * (LOG2E * HEAD_DIM ** -0.5)
        outs[3 * j][0] = p.astype(BF16)
        for c in range(SLABS):
            pbuf[j, c] = p[:, c * LANES:(c + 1) * LANES]
        for di in (1, 2):
            dil = DILATIONS[di]
            o = outs[3 * j + di]
            for r in range(dil):
                for c in range(SLABS):
                    lo = r * ATTN_W + c * LANES
                    o[0, :, lo:lo + LANES] = pbuf[j, c, pl.ds(r, tm // dil, stride=dil), :].astype(BF16)
    gb_ref[...] = proj(3).astype(BF16)
    z_ref[...] = (proj(4) * proj(5)).astype(BF16)


def _proj(x2, g, w_in_bf, tm, batch, seq):
    t, d = x2.shape
    tps = seq // tm
    row = pl.BlockSpec((tm, ATTN_W), lambda i: (i, 0))
    view_specs, view_shapes = [], []
    for _ in range(3):
        for dil in DILATIONS:
            view_specs.append(pl.BlockSpec((1, tm // dil, dil * ATTN_W), lambda i: (i // tps, i % tps, 0)))
            view_shapes.append(jax.ShapeDtypeStruct((batch, seq // dil, dil * ATTN_W), BF16))
    flat = jax.ShapeDtypeStruct((t, ATTN_W), BF16)
    return pl.pallas_call(
        functools.partial(_proj_kernel, tm=tm),
        grid=(t // tm,),
        in_specs=[pl.BlockSpec((tm, d), lambda i: (i, 0)),
                  pl.BlockSpec((1, d), lambda i: (0, 0)),
                  pl.BlockSpec(w_in_bf.shape, lambda i: (0, 0))],
        out_specs=view_specs + [row, row],
        out_shape=view_shapes + [flat, flat],
        scratch_shapes=[pltpu.VMEM((3, SLABS, tm, LANES), F32)],
        compiler_params=_cparams("parallel"),
        name="proj",
    )(x2, g, w_in_bf)


def _bias_tables(dil):
    row = np.arange(Q_SUB)[:, None]
    col = np.arange(K_WIN)[None, :]
    rel = col - SIDE - row
    band = np.abs(rel) <= SIDE
    slopes = 2.0 ** (-8.0 * np.arange(1, N_HEADS + 1) / N_HEADS)
    tabs = []
    for t in range(4):
        ok = band.copy()
        if t & 1:
            ok &= col >= SIDE
        if t & 2:
            ok &= col < SIDE + Q_SUB
        for h in range(N_HEADS):
            tabs.append(np.where(ok, -slopes[h] * np.abs(rel) * dil * LOG2E, NEG))
    return jnp.asarray(np.stack(tabs), dtype=F32)


def _attn_kernel(q_ref, kp_ref, km_ref, kn_ref, vp_ref, vm_ref, vn_ref, bias_ref,
                 o_ref, st_ref, kbuf, vbuf, *, tq, n_sub_total):
    i = pl.program_id(2)
    pieces = ((0, SIDE, kp_ref, vp_ref), (SIDE, SIDE + tq, km_ref, vm_ref), (SIDE + tq, 2 * SIDE + tq, kn_ref, vn_ref))
    for lo_row, hi_row, k_ref, v_ref in pieces:
        kbuf[lo_row:hi_row] = k_ref[0]
        for hp in range(N_HEADS // 2):
            vbuf[lo_row:hi_row, 2 * hp * LANES:(2 * hp + 1) * LANES] = v_ref[0, :, hp * LANES:(hp + 1) * LANES]

    @pl.when((pl.program_id(0) == 0) & (pl.program_id(1) == 0) & (i == 0))
    def _():
        for hp in range(N_HEADS // 2):
            vbuf[:, (2 * hp + 1) * LANES:(2 * hp + 2) * LANES] = jnp.ones((tq + 2 * SIDE, LANES), BF16)

    lane = lax.broadcasted_iota(I32, (Q_SUB, LANES), 1)
    lo = lane < HEAD_DIM
    grp = lane // STAT_LANES
    n_sub = tq // Q_SUB

    def sub(j, carry):
        r0 = pl.multiple_of(j * Q_SUB, Q_SUB)
        g_sub = i * n_sub + j
        tbl = (g_sub == 0).astype(I32) + 2 * (g_sub == n_sub_total - 1).astype(I32)
        m_tile = jnp.zeros((Q_SUB, LANES), F32)
        l_tile = jnp.zeros((Q_SUB, LANES), F32)
        for hp in range(N_HEADS // 2):
            cols = slice(hp * LANES, (hp + 1) * LANES)
            qp = q_ref[0, pl.ds(r0, Q_SUB), cols]
            kw = kbuf[pl.ds(r0, K_WIN), cols]
            vw = vbuf[pl.ds(r0, K_WIN), 2 * hp * LANES:(2 * hp + 2) * LANES]
            outs = []
            for par in range(2):
                qm = jnp.where(lo if par == 0 else jnp.logical_not(lo), qp, jnp.zeros_like(qp))
                s = lax.dot_general(qm, kw, (((1,), (1,)), ((), ())), preferred_element_type=F32)
                s = s + bias_ref[tbl * N_HEADS + 2 * hp + par]
                m = jnp.max(s, axis=1, keepdims=True)
                p = jnp.exp2(s - m).astype(BF16)
                pvl = jnp.dot(p, vw, preferred_element_type=F32)
                outs.append(pvl[:, :LANES])
                m_tile = jnp.where(grp == 2 * hp + par, m, m_tile)
                l_tile = jnp.where(grp == 2 * hp + par, pvl[:, LANES:], l_tile)
            o_ref[0, pl.ds(r0, Q_SUB), cols] = jnp.where(lo, outs[0], outs[1]).astype(BF16)
        st_ref[0, pl.ds(r0, Q_SUB), 0:LANES] = m_tile
        st_ref[0, pl.ds(r0, Q_SUB), LANES:2 * LANES] = l_tile
        return carry

    lax.fori_loop(0, n_sub, sub, 0, unroll=min(n_sub, 4))


def _attn_branch(qv, kv, vv, dil):
    batch, sd, _ = qv.shape
    tq = min(1024, sd)
    n_tiles = sd // tq
    bias = _bias_tables(dil)
    r_side = tq // SIDE
    last_side = sd // SIDE - 1

    main = pl.BlockSpec((1, tq, ATTN_W), lambda b, r, i: (b, i, r))
    prev = pl.BlockSpec((1, SIDE, ATTN_W), lambda b, r, i: (b, jnp.maximum(i * r_side - 1, 0), r))
    nxt = pl.BlockSpec((1, SIDE, ATTN_W), lambda b, r, i: (b, jnp.minimum((i + 1) * r_side, last_side), r))
    st_spec = pl.BlockSpec((1, tq, 2 * LANES), lambda b, r, i: (b, i, r))
    return pl.pallas_call(
        functools.partial(_attn_kernel, tq=tq, n_sub_total=sd // Q_SUB),
        grid=(batch, dil, n_tiles),
        in_specs=[main, prev, main, nxt, prev, main, nxt,
                  pl.BlockSpec(bias.shape, lambda b, r, i: (0, 0, 0))],
        out_specs=[main, st_spec],
        out_shape=[jax.ShapeDtypeStruct((batch, sd, dil * ATTN_W), BF16),
                   jax.ShapeDtypeStruct((batch, sd, dil * 2 * LANES), F32)],
        scratch_shapes=[pltpu.VMEM((tq + 2 * SIDE, ATTN_W), BF16),
                        pltpu.VMEM((tq + 2 * SIDE, 2 * ATTN_W), BF16)],
        compiler_params=_cparams("arbitrary", "arbitrary", "arbitrary"),
        name=f"attn_d{dil}",
    )(qv, kv, kv, kv, vv, vv, vv, bias)


def _mix_kernel(x_ref, o1_ref, o4_ref, o16_ref, l1_ref, l4_ref, l16_ref, gb_ref, z_ref, zp_ref, zn_ref,
                cw_ref, ga_ref, gc_ref, wo_ref, gf_ref, wr2_ref, br_ref, upper_ref, expand_ref,
                x1_ref, h2_ref, ri_ref, rg_ref, cnt_ref, onat, lnat, carry, *, tm, tiles_per_seq):
    i = pl.program_id(0)

    @pl.when(i == 0)
    def _():
        carry[...] = jnp.zeros_like(carry)

    for bi, (o_ref, l_ref, dil) in enumerate(((o4_ref, l4_ref, 4), (o16_ref, l16_ref, 16))):
        n = tm // dil
        for r in range(dil):
            for c in range(2):
                lo = (2 * r + c) * LANES
                lnat[bi, c, pl.ds(r, n, stride=dil), :] = l_ref[0, :, lo:lo + LANES]
            for c in range(SLABS):
                lo = r * ATTN_W + c * LANES
                onat[bi, c, pl.ds(r, n, stride=dil), :] = o_ref[0, :, lo:lo + LANES].astype(F32)

    ms = [l1_ref[0, :, 0:LANES], lnat[0, 0], lnat[1, 0]]
    ls = [l1_ref[0, :, LANES:2 * LANES], lnat[0, 1], lnat[1, 1]]
    mx = jnp.maximum(jnp.maximum(ms[0], ms[1]), ms[2])
    w = [jnp.exp2(t - mx) for t in ms]
    inv = 1.0 / (w[0] * ls[0] + w[1] * ls[1] + w[2] * ls[2])
    wide = []
    for g in range(3):
        hi_lo = jnp.concatenate(_split_bf16(w[g] * inv), axis=1)
        wide.append(jnp.dot(hi_lo, expand_ref[...], preferred_element_type=F32))
    slabs = []
    for c in range(SLABS):
        cs = slice(c * LANES, (c + 1) * LANES)
        slabs.append(wide[0][:, cs] * o1_ref[0, :, cs].astype(F32)
                     + wide[1][:, cs] * onat[0, c] + wide[2][:, cs] * onat[1, c])
    attn = jnp.concatenate(slabs, axis=1)

    z = z_ref[...].astype(F32)
    row = lax.broadcasted_iota(I32, (tm, 1), 0)
    seq_first = (i % tiles_per_seq) == 0
    seq_last = (i % tiles_per_seq) == tiles_per_seq - 1
    z_before = jnp.where(seq_first, 0.0, zp_ref[BF16_ROWS - 1:BF16_ROWS, :].astype(F32))
    z_after = jnp.where(seq_last, 0.0, zn_ref[0:1, :].astype(F32))
    z_up = jnp.where(row == 0, z_before, pltpu.roll(z, 1, 0))
    z_dn = jnp.where(row == tm - 1, z_after, pltpu.roll(z, tm - 1, 0))
    conv = gb_ref[...].astype(F32) * (cw_ref[0:1, :] * z_up + cw_ref[1:2, :] * z + cw_ref[2:3, :] * z_dn)

    na = _rms(attn, ga_ref[...]).astype(BF16)
    nc = _rms(conv, gc_ref[...]).astype(BF16)
    mix = (jnp.dot(na, wo_ref[0:ATTN_W, :], preferred_element_type=F32)
           + jnp.dot(nc, wo_ref[ATTN_W:, :], preferred_element_type=F32))
    x1 = x_ref[...] + mix
    x1_ref[...] = x1
    h2 = _rms(x1, gf_ref[...])
    _store_token_tiles(h2_ref, h2, tm)

    hh, hl = _split_bf16(h2)
    both = jnp.dot(hh, wr2_ref[...], preferred_element_type=F32)
    logits = (both[:, :LANES] + both[:, LANES:]
              + jnp.dot(hl, wr2_ref[:, :LANES], preferred_element_type=F32) + br_ref[...])
    lt = logits.T[0:N_EXPERTS, :]

    erow = lax.broadcasted_iota(I32, (N_EXPERTS, tm), 0)
    work = lt
    idx, val = [], []
    for _ in range(TOP_K):
        mk = jnp.max(work, axis=0, keepdims=True)
        ik = jnp.min(jnp.where(work == mk, erow, N_EXPERTS), axis=0, keepdims=True)
        idx.append(ik)
        val.append(mk)
        work = jnp.where(erow == ik, 2 * NEG, work)
    ex = [jnp.exp(v - val[0]) for v in val]
    ginv = 1.0 / (ex[0] + ex[1] + ex[2] + ex[3])
    gates = [e * ginv for e in ex]

    hit = [erow == ik for ik in idx]
    onehot = (hit[0] | hit[1] | hit[2] | hit[3]).astype(F32)
    before = jnp.dot(onehot.astype(BF16), upper_ref[...], preferred_element_type=F32) + carry[:, 0:1]
    ranks = [jnp.sum(jnp.where(h, before, 0.0), axis=0, keepdims=True) for h in hit]
    new_carry = carry[:, 0:1] + jnp.sum(onehot, axis=1, keepdims=True)
    carry[...] = jnp.broadcast_to(new_carry, carry.shape)
    cnt_ref[...] = jnp.broadcast_to(new_carry, cnt_ref.shape)

    ri_ref[...] = jnp.concatenate(idx + [r.astype(I32) for r in ranks], axis=0)
    rg_ref[...] = jnp.concatenate(gates + [jnp.zeros((TOP_K, tm), F32)], axis=0)


def _mix(x2, o_views, l_views, gb, z, conv_w, ga, gc, w_out_bf, gf, w_router, b_router, tm, batch, seq):
    t, d = x2.shape
    n = t // tm
    tps = seq // tm
    upper = jnp.asarray(np.triu(np.ones((tm, tm), np.float32), 1), dtype=BF16)
    sel = np.arange(LANES)[:, None] == STAT_LANES * (np.arange(ATTN_W)[None, :] // HEAD_DIM)
    expand = jnp.asarray(np.concatenate([sel, sel], axis=0), dtype=BF16)
    wr_pad = jnp.zeros((d, LANES), F32).at[:, :N_EXPERTS].set(w_router)
    wr2 = jnp.concatenate(_split_bf16(wr_pad), axis=1)
    br_pad = jnp.zeros((1, LANES), F32).at[0, :N_EXPERTS].set(b_router)

    rowd = pl.BlockSpec((tm, d), lambda i: (i, 0))
    rowa = pl.BlockSpec((tm, ATTN_W), lambda i: (i, 0))
    halo = tm // BF16_ROWS
    zprev = pl.BlockSpec((BF16_ROWS, CONV_W), lambda i: (jnp.maximum(i * halo - 1, 0), 0))
    znext = pl.BlockSpec((BF16_ROWS, CONV_W), lambda i: (jnp.minimum((i + 1) * halo, t // BF16_ROWS - 1), 0))

    def view(width, dil):
        return pl.BlockSpec((1, tm // dil, dil * width), lambda i: (i // tps, i % tps, 0))

    def full(a):
        return pl.BlockSpec(a.shape, lambda i: (0,) * a.ndim)

    consts = (conv_w, ga, gc, w_out_bf, gf, wr2, br_pad, upper, expand)
    return pl.pallas_call(
        functools.partial(_mix_kernel, tm=tm, tiles_per_seq=tps),
        grid=(n,),
        in_specs=[rowd] + [view(ATTN_W, dil) for dil in DILATIONS] + [view(2 * LANES, dil) for dil in DILATIONS]
                 + [rowa, rowa, zprev, znext] + [full(a) for a in consts],
        out_specs=[rowd, pl.BlockSpec((tm * SUBLANES, LANES), lambda i: (i, 0)),
                   pl.BlockSpec((ROUTE_ROWS, tm), lambda i: (0, i)), pl.BlockSpec((ROUTE_ROWS, tm), lambda i: (0, i)),
                   pl.BlockSpec((N_EXPERTS, LANES), lambda i: (0, 0))],
        out_shape=[jax.ShapeDtypeStruct((t, d), F32), jax.ShapeDtypeStruct((t * SUBLANES, LANES), F32),
                   jax.ShapeDtypeStruct((ROUTE_ROWS, t), I32), jax.ShapeDtypeStruct((ROUTE_ROWS, t), F32),
                   jax.ShapeDtypeStruct((N_EXPERTS, LANES), F32)],
        scratch_shapes=[pltpu.VMEM((2, SLABS, tm, LANES), F32), pltpu.VMEM((2, 2, tm, LANES), F32),
                        pltpu.VMEM((N_EXPERTS, LANES), F32)],
        compiler_params=_cparams("arbitrary"),
        name="mix_router",
    )(x2, *o_views, *l_views, gb, z, z, z, *consts)


def _dispatch_kernel(zrow_ref, dest_ref, h_ref, xs_ref, slot_smem, zero_buf, sem_idx, sem_row, sem_zero,
                     *, tm, tb, n_tiles):
    i = pl.program_id(0)
    n = tm * TOP_K

    def idx_copy(tile, buf):
        return pltpu.make_async_copy(dest_ref.at[pl.ds(tile * n, n)],
                                     slot_smem.at[pl.ds(pl.multiple_of(buf * n, n), n)], sem_idx.at[buf])

    @pl.when(i == 0)
    def _():
        zero_buf[...] = jnp.zeros_like(zero_buf)
        for j in range(2 * N_EXPERTS):
            @pl.when(zrow_ref[j] >= 0)
            def _():
                start = pl.multiple_of(zrow_ref[j] * SUBLANES, tb * SUBLANES)
                pltpu.make_async_copy(zero_buf, xs_ref.at[pl.ds(start, tb * SUBLANES)], sem_zero).start()
        for j in range(2 * N_EXPERTS):
            @pl.when(zrow_ref[j] >= 0)
            def _():
                pltpu.make_async_copy(zero_buf, xs_ref.at[pl.ds(0, tb * SUBLANES)], sem_zero).wait()
        idx_copy(0, 0).start()

    cur = i % 2
    idx_copy(i, cur).wait()

    @pl.when(i + 1 < n_tiles)
    def _():
        idx_copy(i + 1, 1 - cur).start()

    bases = [cur * n + k * tm for k in range(TOP_K)]

    def body(t, c):
        src = h_ref.at[pl.ds(pl.multiple_of(t * SUBLANES, SUBLANES), SUBLANES)]
        for k in range(TOP_K):
            row = pl.multiple_of(slot_smem[bases[k] + t], SUBLANES)
            pltpu.make_async_copy(src, xs_ref.at[pl.ds(row, SUBLANES)], sem_row).start(priority=k % 2)
        return c

    lax.fori_loop(0, tm, body, 0, unroll=ISSUE_GROUP)

    for _ in range(TOP_K):
        pltpu.make_async_copy(h_ref, xs_ref.at[pl.ds(0, tm * SUBLANES)], sem_row).wait()


def _dispatch(zrow, dest_tiles, h2_tiles, n_slots, tm, tb):
    t = h2_tiles.shape[0] // SUBLANES
    n_tiles = t // tm
    grid_spec = pltpu.PrefetchScalarGridSpec(
        num_scalar_prefetch=1,
        grid=(n_tiles,),
        in_specs=[pl.BlockSpec(memory_space=pl.ANY),
                  pl.BlockSpec((tm * SUBLANES, LANES), lambda i, zr: (i, 0))],
        out_specs=pl.BlockSpec(memory_space=pl.ANY),
        scratch_shapes=[pltpu.SMEM((2 * tm * TOP_K,), I32), pltpu.VMEM((tb * SUBLANES, LANES), F32),
                        pltpu.SemaphoreType.DMA((2,)), pltpu.SemaphoreType.DMA, pltpu.SemaphoreType.DMA],
    )
    return pl.pallas_call(
        functools.partial(_dispatch_kernel, tm=tm, tb=tb, n_tiles=n_tiles),
        grid_spec=grid_spec,
        out_shape=jax.ShapeDtypeStruct((n_slots * SUBLANES, LANES), F32),
        compiler_params=_cparams("arbitrary"),
        name="dispatch",
    )(zrow, dest_tiles, h2_tiles)


def _expert_kernel(be_ref, bx_ref, bv_ref, xs_ref, wgu_ref, bgu_ref, wd_ref, bd_ref, o_ref, wgu_bf, wd_bf,
                   *, tb, dff):
    i = pl.program_id(0)

    @pl.when(jnp.logical_or(i == 0, be_ref[i] != be_ref[jnp.maximum(i - 1, 0)]))
    def _():
        wgu_bf[...] = wgu_ref[0].astype(BF16)
        wd_bf[...] = wd_ref[0].astype(BF16)

    @pl.when(bv_ref[i] > 0)
    def _():
        x = _load_token_tiles(xs_ref, tb).astype(BF16)
        hu = jnp.dot(x, wgu_bf[...], preferred_element_type=F32) + bgu_ref[0]
        a = jnp.minimum(hu[:, :dff], SWIGLU_LIMIT)
        lin = jnp.clip(hu[:, dff:], -SWIGLU_LIMIT, SWIGLU_LIMIT)
        act = a * (1.0 / (1.0 + jnp.exp(-SWIGLU_ALPHA * a))) * (lin + 1.0)
        out = jnp.dot(act.astype(BF16), wd_bf[...], preferred_element_type=F32) + bd_ref[0]
        _store_token_tiles(o_ref, out, tb)

    @pl.when(bv_ref[i] == 0)
    def _():
        o_ref[...] = jnp.zeros_like(o_ref)


def _experts(blk_e, blk_x, blk_v, xs_tiles, wgu, bgu, wd, bd, tb):
    d, dff = wd.shape[2], wd.shape[1]
    n_slots = xs_tiles.shape[0] // SUBLANES
    nb = n_slots // tb
    rows = tb * SUBLANES
    grid_spec = pltpu.PrefetchScalarGridSpec(
        num_scalar_prefetch=3,
        grid=(nb,),
        in_specs=[pl.BlockSpec((rows, LANES), lambda i, be, bx, bv: (bx[i], 0)),
                  pl.BlockSpec((1, d, 2 * dff), lambda i, be, bx, bv: (be[i], 0, 0)),
                  pl.BlockSpec((1, 1, 2 * dff), lambda i, be, bx, bv: (be[i], 0, 0)),
                  pl.BlockSpec((1, dff, d), lambda i, be, bx, bv: (be[i], 0, 0)),
                  pl.BlockSpec((1, 1, d), lambda i, be, bx, bv: (be[i], 0, 0))],
        out_specs=pl.BlockSpec((rows, LANES), lambda i, be, bx, bv: (i, 0)),
        scratch_shapes=[pltpu.VMEM((d, 2 * dff), BF16), pltpu.VMEM((dff, d), BF16)],
    )
    return pl.pallas_call(
        functools.partial(_expert_kernel, tb=tb, dff=dff),
        grid_spec=grid_spec,
        out_shape=jax.ShapeDtypeStruct((n_slots * SUBLANES, LANES), F32),
        compiler_params=_cparams("arbitrary"),
        name="experts",
    )(blk_e, blk_x, blk_v, xs_tiles, wgu, bgu, wd, bd)


def _combine_kernel(dest_ref, x1_ref, rg_ref, fg_ref, mo_ref, out_ref, slot_smem, gbuf, sem_idx, sem_row,
                    *, tm, n_tiles):
    i = pl.program_id(0)
    n = tm * TOP_K

    def idx_copy(tile):
        buf = tile % IDX_RING
        return pltpu.make_async_copy(dest_ref.at[pl.ds(tile * n, n)],
                                     slot_smem.at[pl.ds(pl.multiple_of(buf * n, n), n)], sem_idx.at[buf])

    def issue(tile):
        gb = tile % GATHER_RING
        bases = [(tile % IDX_RING) * n + k * tm for k in range(TOP_K)]
        dsts = [gbuf.at[gb, k] for k in range(TOP_K)]
        sem = sem_row.at[gb]

        def body(t, c):
            dst = pl.ds(pl.multiple_of(t * SUBLANES, SUBLANES), SUBLANES)
            for k in range(TOP_K):
                row = pl.multiple_of(slot_smem[bases[k] + t], SUBLANES)
                pltpu.make_async_copy(mo_ref.at[pl.ds(row, SUBLANES)], dsts[k].at[dst], sem).start(priority=k % 2)
            return c

        lax.fori_loop(0, tm, body, 0, unroll=ISSUE_GROUP)

    @pl.when(i == 0)
    def _():
        idx_copy(0).start()
        if n_tiles > 1:
            idx_copy(1).start()
        idx_copy(0).wait()
        issue(0)

    @pl.when(i + 1 < n_tiles)
    def _():
        idx_copy(i + 1).wait()
        issue(i + 1)

    @pl.when(i + 2 < n_tiles)
    def _():
        idx_copy(i + 2).start()

    cur = i % GATHER_RING
    for k in range(TOP_K):
        pltpu.make_async_copy(mo_ref.at[pl.ds(0, tm * SUBLANES)], gbuf.at[cur, k], sem_row.at[cur]).wait()

    g_rows = jnp.concatenate([rg_ref[...], jnp.zeros((LANES - ROUTE_ROWS, tm), F32)], axis=0)
    g = g_rows.T
    y = x1_ref[...]
    for k in range(TOP_K):
        y = y + g[:, k:k + 1] * _load_token_tiles(gbuf.at[cur, k], tm)
    out_ref[...] = _rms(y, fg_ref[...])


def _combine(dest_tiles, x1, rg, fg, mlp_out, tm):
    t, d = x1.shape
    n_tiles = t // tm
    rowd = pl.BlockSpec((tm, d), lambda i: (i, 0))
    return pl.pallas_call(
        functools.partial(_combine_kernel, tm=tm, n_tiles=n_tiles),
        grid=(n_tiles,),
        in_specs=[pl.BlockSpec(memory_space=pl.ANY), rowd,
                  pl.BlockSpec((ROUTE_ROWS, tm), lambda i: (0, i)),
                  pl.BlockSpec((1, d), lambda i: (0, 0)),
                  pl.BlockSpec(memory_space=pl.ANY)],
        out_specs=rowd,
        out_shape=jax.ShapeDtypeStruct((t, d), F32),
        scratch_shapes=[pltpu.SMEM((IDX_RING * tm * TOP_K,), I32),
                        pltpu.VMEM((GATHER_RING, TOP_K, tm * SUBLANES, LANES), F32),
                        pltpu.SemaphoreType.DMA((IDX_RING,)), pltpu.SemaphoreType.DMA((GATHER_RING,))],
        compiler_params=_cparams("arbitrary"),
        name="combine",
    )(dest_tiles, x1, rg, fg, mlp_out)


def _block_tables(counts, idx, rank, tb, nb):
    padded = (counts + tb - 1) // tb * tb
    pad_end = jnp.cumsum(padded)
    pad_start = pad_end - padded
    sel = idx[..., None] == jnp.arange(N_EXPERTS, dtype=I32)
    dest = jnp.sum(jnp.where(sel, pad_start, 0), axis=-1) + rank
    blocks_e = padded // tb
    blk_end = jnp.cumsum(blocks_e)
    total = blk_end[-1]
    bid = jnp.arange(nb, dtype=I32)
    src = jnp.minimum(bid, total - 1)
    e = jnp.minimum(jnp.sum((src[:, None] >= blk_end[None, :]).astype(I32), axis=1), N_EXPERTS - 1)
    first = jnp.sum(jnp.where(e[:, None] == jnp.arange(N_EXPERTS), blk_end - blocks_e, 0), axis=1)
    cnt = jnp.sum(jnp.where(e[:, None] == jnp.arange(N_EXPERTS), counts, 0), axis=1)
    valid = jnp.where(bid < total, jnp.clip(cnt - (src - first) * tb, 0, tb), 0)
    tail = total + jnp.arange(N_EXPERTS, dtype=I32)
    zrow = jnp.concatenate([jnp.where(padded > 0, pad_end - tb, -1), jnp.where(tail < nb, tail * tb, -1)])
    return dest.astype(I32), e.astype(I32), src.astype(I32), valid.astype(I32), zrow.astype(I32)


def _tile_major(dest, tm):
    k, t = dest.shape
    return (dest * SUBLANES).reshape(k, t // tm, tm).transpose(1, 0, 2).reshape(-1)


def kernel(x, mix_norm_g, w_in, conv_w, attn_out_norm_g, conv_out_norm_g, w_out, ffn_norm_g, w_router,
           b_router, w_gate_up, b_gate_up, w_down, b_down, final_norm_g):
    batch, seq, d = x.shape
    t = batch * seq
    assert w_in.shape[0] == 1, "single-layer trunk: the final RMSNorm is fused into the combine kernel"
    assert d == SUBLANES * LANES, "one token row must fill exactly one (8, 128) f32 tile"
    tm = 512
    tb = 512
    tc = 512
    td = 1024
    nb = (t * TOP_K) // tb + N_EXPERTS
    x2 = x.reshape(t, d)

    *qkv, gb, z = _proj(x2, mix_norm_g[0][None, :], w_in[0].astype(BF16), tm, batch, seq)
    o_views, l_views = [], []
    for di, dil in enumerate(DILATIONS):
        o, lse = _attn_branch(qkv[di], qkv[3 + di], qkv[6 + di], dil)
        o_views.append(o)
        l_views.append(lse)

    x1, h2, ri, rg, cnt = _mix(x2, o_views, l_views, gb, z, conv_w[0], attn_out_norm_g[0][None, :],
                               conv_out_norm_g[0][None, :], w_out[0].astype(BF16), ffn_norm_g[0][None, :],
                               w_router[0], b_router[0], tm, batch, seq)

    counts = cnt[:, 0].astype(I32)
    dest, blk_e, blk_x, blk_v, zrow = _block_tables(counts, ri[:TOP_K], ri[TOP_K:], tb, nb)

    xs = _dispatch(zrow, _tile_major(dest, td), h2, nb * tb, td, tb)
    mo = _experts(blk_e, blk_x, blk_v, xs, w_gate_up[0], b_gate_up[0][:, None, :],
                  w_down[0], b_down[0][:, None, :], tb)
    out = _combine(_tile_major(dest, tc), x1, rg, final_norm_g[None, :], mo, tc)
    return out.reshape(batch, seq, d)
```

```python
import functools

import numpy as np
import jax
import jax.numpy as jnp
from jax import lax
from jax.experimental import pallas as pl
from jax.experimental.pallas import tpu as pltpu

F32 = jnp.float32
BF16 = jnp.bfloat16
I32 = jnp.int32

HEAD_DIM = 64
N_HEADS = 8
ATTN_W = N_HEADS * HEAD_DIM
CONV_W = 512
N_EXPERTS = 32
TOP_K = 4
DILATIONS = (1, 4, 16)
SIDE = 64
SWIGLU_ALPHA = 1.702
SWIGLU_LIMIT = 7.0
RMS_EPS = 1e-5
NEG = -1e30
LOG2E = 1.4426950408889634

LANES = 128
SLABS = ATTN_W // LANES
Q_SUB = 128
K_WIN = Q_SUB + 2 * SIDE
VMEM_LIMIT = 56 * 1024 * 1024


def _cparams(*sem):
    return pltpu.CompilerParams(dimension_semantics=sem, vmem_limit_bytes=VMEM_LIMIT)


def _rms(x, g):
    return x * lax.rsqrt(jnp.mean(x * x, axis=-1, keepdims=True) + RMS_EPS) * g


SUBLANES = 8
BF16_ROWS = 16
STAT_LANES = LANES // N_HEADS
ROUTE_ROWS = 2 * TOP_K
ISSUE_GROUP = 8
STAGE_RING = 3
GATHER_RING = 2
IDX_RING = GATHER_RING + 1


def _store_token_tiles(ref, x, n):
    for s in range(x.shape[1] // LANES):
        ref[pl.ds(s, n, stride=SUBLANES), :] = x[:, s * LANES:(s + 1) * LANES]


def _load_token_tiles(ref, n):
    return jnp.concatenate([ref[pl.ds(s, n, stride=SUBLANES), :] for s in range(SUBLANES)], axis=1)


def _split_bf16(a):
    hi = a.astype(BF16)
    return hi, (a - hi.astype(F32)).astype(BF16)


def _proj_kernel(x_ref, g_ref, w_ref, *refs, tm):
    outs = refs[:9]
    gb_ref, z_ref, pbuf = refs[9:]
    h = _rms(x_ref[...], g_ref[...]).astype(BF16)

    def proj(j):
        return jnp.dot(h, w_ref[:, j * ATTN_W:(j + 1) * ATTN_W], preferred_element_type=F32)

    for j in range(3):
        p = proj(j)
        if j == 0:
            p = p * (LOG2E * HEAD_DIM ** -0.5)
        outs[3 * j][0] = p.astype(BF16)
        for c in range(SLABS):
            pbuf[j, c] = p[:, c * LANES:(c + 1) * LANES]
        for di in (1, 2):
            dil = DILATIONS[di]
            o = outs[3 * j + di]
            for r in range(dil):
                for c in range(SLABS):
                    lo = r * ATTN_W + c * LANES
                    o[0, :, lo:lo + LANES] = pbuf[j, c, pl.ds(r, tm // dil, stride=dil), :].astype(BF16)
    gb_ref[...] = proj(3).astype(BF16)
    z_ref[...] = (proj(4) * proj(5)).astype(BF16)


def _proj(x2, g, w_in_bf, tm, batch, seq):
    t, d = x2.shape
    tps = seq // tm
    row = pl.BlockSpec((tm, ATTN_W), lambda i: (i, 0))
    view_specs, view_shapes = [], []
    for _ in range(3):
        for dil in DILATIONS:
            view_specs.append(pl.BlockSpec((1, tm // dil, dil * ATTN_W), lambda i: (i // tps, i % tps, 0)))
            view_shapes.append(jax.ShapeDtypeStruct((batch, seq // dil, dil * ATTN_W), BF16))
    flat = jax.ShapeDtypeStruct((t, ATTN_W), BF16)
    return pl.pallas_call(
        functools.partial(_proj_kernel, tm=tm),
        grid=(t // tm,),
        in_specs=[pl.BlockSpec((tm, d), lambda i: (i, 0)),
                  pl.BlockSpec((1, d), lambda i: (0, 0)),
                  pl.BlockSpec(w_in_bf.shape, lambda i: (0, 0))],
        out_specs=view_specs + [row, row],
        out_shape=view_shapes + [flat, flat],
        scratch_shapes=[pltpu.VMEM((3, SLABS, tm, LANES), F32)],
        compiler_params=_cparams("parallel"),
        name="proj",
    )(x2, g, w_in_bf)


def _bias_tables(dil):
    row = np.arange(Q_SUB)[:, None]
    col = np.arange(K_WIN)[None, :]
    rel = col - SIDE - row
    band = np.abs(rel) <= SIDE
    slopes = 2.0 ** (-8.0 * np.arange(1, N_HEADS + 1) / N_HEADS)
    tabs = []
    for t in range(4):
        ok = band.copy()
        if t & 1:
            ok &= col >= SIDE
        if t & 2:
            ok &= col < SIDE + Q_SUB
        for h in range(N_HEADS):
            tabs.append(np.where(ok, -slopes[h] * np.abs(rel) * dil * LOG2E, NEG))
    return jnp.asarray(np.stack(tabs), dtype=F32)


def _attn_kernel(q_ref, kp_ref, km_ref, kn_ref, vp_ref, vm_ref, vn_ref, bias_ref,
                 o_ref, st_ref, kbuf, vbuf, *, tq, n_sub_total):
    i = pl.program_id(2)
    pieces = ((0, SIDE, kp_ref, vp_ref), (SIDE, SIDE + tq, km_ref, vm_ref), (SIDE + tq, 2 * SIDE + tq, kn_ref, vn_ref))
    for lo_row, hi_row, k_ref, v_ref in pieces:
        kbuf[lo_row:hi_row] = k_ref[0]
        for hp in range(N_HEADS // 2):
            vbuf[lo_row:hi_row, 2 * hp * LANES:(2 * hp + 1) * LANES] = v_ref[0, :, hp * LANES:(hp + 1) * LANES]

    @pl.when((pl.program_id(0) == 0) & (pl.program_id(1) == 0) & (i == 0))
    def _():
        for hp in range(N_HEADS // 2):
            vbuf[:, (2 * hp + 1) * LANES:(2 * hp + 2) * LANES] = jnp.ones((tq + 2 * SIDE, LANES), BF16)

    lane = lax.broadcasted_iota(I32, (Q_SUB, LANES), 1)
    lo = lane < HEAD_DIM
    grp = lane // STAT_LANES
    n_sub = tq // Q_SUB

    def sub(j, carry):
        r0 = pl.multiple_of(j * Q_SUB, Q_SUB)
        g_sub = i * n_sub + j
        tbl = (g_sub == 0).astype(I32) + 2 * (g_sub == n_sub_total - 1).astype(I32)
        m_tile = jnp.zeros((Q_SUB, LANES), F32)
        l_tile = jnp.zeros((Q_SUB, LANES), F32)
        for hp in range(N_HEADS // 2):
            cols = slice(hp * LANES, (hp + 1) * LANES)
            qp = q_ref[0, pl.ds(r0, Q_SUB), cols]
            kw = kbuf[pl.ds(r0, K_WIN), cols]
            vw = vbuf[pl.ds(r0, K_WIN), 2 * hp * LANES:(2 * hp + 2) * LANES]
            outs = []
            for par in range(2):
                qm = jnp.where(lo if par == 0 else jnp.logical_not(lo), qp, jnp.zeros_like(qp))
                s = lax.dot_general(qm, kw, (((1,), (1,)), ((), ())), preferred_element_type=F32)
                s = s + bias_ref[tbl * N_HEADS + 2 * hp + par]
                m = jnp.max(s, axis=1, keepdims=True)
                p = jnp.exp2(s - m).astype(BF16)
                pvl = jnp.dot(p, vw, preferred_element_type=F32)
                outs.append(pvl[:, :LANES])
                m_tile = jnp.where(grp == 2 * hp + par, m, m_tile)
                l_tile = jnp.where(grp == 2 * hp + par, pvl[:, LANES:], l_tile)
            o_ref[0, pl.ds(r0, Q_SUB), cols] = jnp.where(lo, outs[0], outs[1]).astype(BF16)
        st_ref[0, pl.ds(r0, Q_SUB), 0:LANES] = m_tile
        st_ref[0, pl.ds(r0, Q_SUB), LANES:2 * LANES] = l_tile
        return carry

    lax.fori_loop(0, n_sub, sub, 0, unroll=min(n_sub, 4))


def _attn_branch(qv, kv, vv, dil):
    batch, sd, _ = qv.shape
    tq = min(1024, sd)
    n_tiles = sd // tq
    bias = _bias_tables(dil)
    r_side = tq // SIDE
    last_side = sd // SIDE - 1

    main = pl.BlockSpec((1, tq, ATTN_W), lambda b, r, i: (b, i, r))
    prev = pl.BlockSpec((1, SIDE, ATTN_W), lambda b, r, i: (b, jnp.maximum(i * r_side - 1, 0), r))
    nxt = pl.BlockSpec((1, SIDE, ATTN_W), lambda b, r, i: (b, jnp.minimum((i + 1) * r_side, last_side), r))
    st_spec = pl.BlockSpec((1, tq, 2 * LANES), lambda b, r, i: (b, i, r))
    return pl.pallas_call(
        functools.partial(_attn_kernel, tq=tq, n_sub_total=sd // Q_SUB),
        grid=(batch, dil, n_tiles),
        in_specs=[main, prev, main, nxt, prev, main, nxt,
                  pl.BlockSpec(bias.shape, lambda b, r, i: (0, 0, 0))],
        out_specs=[main, st_spec],
        out_shape=[jax.ShapeDtypeStruct((batch, sd, dil * ATTN_W), BF16),
                   jax.ShapeDtypeStruct((batch, sd, dil * 2 * LANES), F32)],
        scratch_shapes=[pltpu.VMEM((tq + 2 * SIDE, ATTN_W), BF16),
                        pltpu.VMEM((tq + 2 * SIDE, 2 * ATTN_W), BF16)],
        compiler_params=_cparams("arbitrary", "arbitrary", "arbitrary"),
        name=f"attn_d{dil}",
    )(qv, kv, kv, kv, vv, vv, vv, bias)


def _mix_kernel(x_ref, o1_ref, o4_ref, o16_ref, l1_ref, l4_ref, l16_ref, gb_ref, z_ref, zp_ref, zn_ref,
                cw_ref, ga_ref, gc_ref, wo_ref, gf_ref, wr2_ref, br_ref, upper_ref, expand_ref,
                x1_ref, h2_ref, ri_ref, rg_ref, cnt_ref, onat, lnat, carry, *, tm, tiles_per_seq):
    i = pl.program_id(0)

    @pl.when(i == 0)
    def _():
        carry[...] = jnp.zeros_like(carry)

    for bi, (o_ref, l_ref, dil) in enumerate(((o4_ref, l4_ref, 4), (o16_ref, l16_ref, 16))):
        n = tm // dil
        for r in range(dil):
            for c in range(2):
                lo = (2 * r + c) * LANES
                lnat[bi, c, pl.ds(r, n, stride=dil), :] = l_ref[0, :, lo:lo + LANES]
            for c in range(SLABS):
                lo = r * ATTN_W + c * LANES
                onat[bi, c, pl.ds(r, n, stride=dil), :] = o_ref[0, :, lo:lo + LANES].astype(F32)

    ms = [l1_ref[0, :, 0:LANES], lnat[0, 0], lnat[1, 0]]
    ls = [l1_ref[0, :, LANES:2 * LANES], lnat[0, 1], lnat[1, 1]]
    mx = jnp.maximum(jnp.maximum(ms[0], ms[1]), ms[2])
    w = [jnp.exp2(t - mx) for t in ms]
    inv = 1.0 / (w[0] * ls[0] + w[1] * ls[1] + w[2] * ls[2])
    wide = []
    for g in range(3):
        hi_lo = jnp.concatenate(_split_bf16(w[g] * inv), axis=1)
        wide.append(jnp.dot(hi_lo, expand_ref[...], preferred_element_type=F32))
    slabs = []
    for c in range(SLABS):
        cs = slice(c * LANES, (c + 1) * LANES)
        slabs.append(wide[0][:, cs] * o1_ref[0, :, cs].astype(F32)
                     + wide[1][:, cs] * onat[0, c] + wide[2][:, cs] * onat[1, c])
    attn = jnp.concatenate(slabs, axis=1)

    z = z_ref[...].astype(F32)
    row = lax.broadcasted_iota(I32, (tm, 1), 0)
    seq_first = (i % tiles_per_seq) == 0
    seq_last = (i % tiles_per_seq) == tiles_per_seq - 1
    z_before = jnp.where(seq_first, 0.0, zp_ref[BF16_ROWS - 1:BF16_ROWS, :].astype(F32))
    z_after = jnp.where(seq_last, 0.0, zn_ref[0:1, :].astype(F32))
    z_up = jnp.where(row == 0, z_before, pltpu.roll(z, 1, 0))
    z_dn = jnp.where(row == tm - 1, z_after, pltpu.roll(z, tm - 1, 0))
    conv = gb_ref[...].astype(F32) * (cw_ref[0:1, :] * z_up + cw_ref[1:2, :] * z + cw_ref[2:3, :] * z_dn)

    na = _rms(attn, ga_ref[...]).astype(BF16)
    nc = _rms(conv, gc_ref[...]).astype(BF16)
    mix = (jnp.dot(na, wo_ref[0:ATTN_W, :], preferred_element_type=F32)
           + jnp.dot(nc, wo_ref[ATTN_W:, :], preferred_element_type=F32))
    x1 = x_ref[...] + mix
    x1_ref[...] = x1
    h2 = _rms(x1, gf_ref[...])
    _store_token_tiles(h2_ref, h2, tm)

    hh, hl = _split_bf16(h2)
    both = jnp.dot(hh, wr2_ref[...], preferred_element_type=F32)
    logits = (both[:, :LANES] + both[:, LANES:]
              + jnp.dot(hl, wr2_ref[:, :LANES], preferred_element_type=F32) + br_ref[...])
    lt = logits.T[0:N_EXPERTS, :]

    erow = lax.broadcasted_iota(I32, (N_EXPERTS, tm), 0)
    work = lt
    idx, val = [], []
    for _ in range(TOP_K):
        mk = jnp.max(work, axis=0, keepdims=True)
        ik = jnp.min(jnp.where(work == mk, erow, N_EXPERTS), axis=0, keepdims=True)
        idx.append(ik)
        val.append(mk)
        work = jnp.where(erow == ik, 2 * NEG, work)
    ex = [jnp.exp(v - val[0]) for v in val]
    ginv = 1.0 / (ex[0] + ex[1] + ex[2] + ex[3])
    gates = [e * ginv for e in ex]

    hit = [erow == ik for ik in idx]
    onehot = (hit[0] | hit[1] | hit[2] | hit[3]).astype(F32)
    before = jnp.dot(onehot.astype(BF16), upper_ref[...], preferred_element_type=F32) + carry[:, 0:1]
    ranks = [jnp.sum(jnp.where(h, before, 0.0), axis=0, keepdims=True) for h in hit]
    new_carry = carry[:, 0:1] + jnp.sum(onehot, axis=1, keepdims=True)
    carry[...] = jnp.broadcast_to(new_carry, carry.shape)
    cnt_ref[...] = jnp.broadcast_to(new_carry, cnt_ref.shape)

    ri_ref[...] = jnp.concatenate(idx + [r.astype(I32) for r in ranks], axis=0)
    rg_ref[...] = jnp.concatenate(gates + [jnp.zeros((TOP_K, tm), F32)], axis=0)


def _mix(x2, o_views, l_views, gb, z, conv_w, ga, gc, w_out_bf, gf, w_router, b_router, tm, batch, seq):
    t, d = x2.shape
    n = t // tm
    tps = seq // tm
    upper = jnp.asarray(np.triu(np.ones((tm, tm), np.float32), 1), dtype=BF16)
    sel = np.arange(LANES)[:, None] == STAT_LANES * (np.arange(ATTN_W)[None, :] // HEAD_DIM)
    expand = jnp.asarray(np.concatenate([sel, sel], axis=0), dtype=BF16)
    wr_pad = jnp.zeros((d, LANES), F32).at[:, :N_EXPERTS].set(w_router)
    wr2 = jnp.concatenate(_split_bf16(wr_pad), axis=1)
    br_pad = jnp.zeros((1, LANES), F32).at[0, :N_EXPERTS].set(b_router)

    rowd = pl.BlockSpec((tm, d), lambda i: (i, 0))
    rowa = pl.BlockSpec((tm, ATTN_W), lambda i: (i, 0))
    halo = tm // BF16_ROWS
    zprev = pl.BlockSpec((BF16_ROWS, CONV_W), lambda i: (jnp.maximum(i * halo - 1, 0), 0))
    znext = pl.BlockSpec((BF16_ROWS, CONV_W), lambda i: (jnp.minimum((i + 1) * halo, t // BF16_ROWS - 1), 0))

    def view(width, dil):
        return pl.BlockSpec((1, tm // dil, dil * width), lambda i: (i // tps, i % tps, 0))

    def full(a):
        return pl.BlockSpec(a.shape, lambda i: (0,) * a.ndim)

    consts = (conv_w, ga, gc, w_out_bf, gf, wr2, br_pad, upper, expand)
    return pl.pallas_call(
        functools.partial(_mix_kernel, tm=tm, tiles_per_seq=tps),
        grid=(n,),
        in_specs=[rowd] + [view(ATTN_W, dil) for dil in DILATIONS] + [view(2 * LANES, dil) for dil in DILATIONS]
                 + [rowa, rowa, zprev, znext] + [full(a) for a in consts],
        out_specs=[rowd, pl.BlockSpec((tm * SUBLANES, LANES), lambda i: (i, 0)),
                   pl.BlockSpec((ROUTE_ROWS, tm), lambda i: (0, i)), pl.BlockSpec((ROUTE_ROWS, tm), lambda i: (0, i)),
                   pl.BlockSpec((N_EXPERTS, LANES), lambda i: (0, 0))],
        out_shape=[jax.ShapeDtypeStruct((t, d), F32), jax.ShapeDtypeStruct((t * SUBLANES, LANES), F32),
                   jax.ShapeDtypeStruct((ROUTE_ROWS, t), I32), jax.ShapeDtypeStruct((ROUTE_ROWS, t), F32),
                   jax.ShapeDtypeStruct((N_EXPERTS, LANES), F32)],
        scratch_shapes=[pltpu.VMEM((2, SLABS, tm, LANES), F32), pltpu.VMEM((2, 2, tm, LANES), F32),
                        pltpu.VMEM((N_EXPERTS, LANES), F32)],
        compiler_params=_cparams("arbitrary"),
        name="mix_router",
    )(x2, *o_views, *l_views, gb, z, z, z, *consts)


def _dispatch_kernel(zrow_ref, dest_ref, h_ref, xs_ref, slot_smem, zero_buf, stage, sem_idx, sem_in, sem_row,
                     sem_zero, *, tm, tb, n_tiles):
    i = pl.program_id(0)
    n = tm * TOP_K
    rows = tm * SUBLANES

    def idx_copy(tile, buf):
        return pltpu.make_async_copy(dest_ref.at[pl.ds(tile * n, n)],
                                     slot_smem.at[pl.ds(pl.multiple_of(buf * n, n), n)], sem_idx.at[buf])

    def load(tile):
        buf = tile % STAGE_RING
        return pltpu.make_async_copy(h_ref.at[pl.ds(pl.multiple_of(tile * rows, rows), rows)], stage.at[buf],
                                     sem_in.at[buf])

    def drain(tile):
        buf = tile % STAGE_RING
        for _ in range(TOP_K):
            pltpu.make_async_copy(stage.at[buf], xs_ref.at[pl.ds(0, rows)], sem_row.at[buf]).wait()

    @pl.when(i == 0)
    def _():
        zero_buf[...] = jnp.zeros_like(zero_buf)
        for j in range(2 * N_EXPERTS):
            @pl.when(zrow_ref[j] >= 0)
            def _():
                start = pl.multiple_of(zrow_ref[j] * SUBLANES, tb * SUBLANES)
                pltpu.make_async_copy(zero_buf, xs_ref.at[pl.ds(start, tb * SUBLANES)], sem_zero).start()
        for j in range(2 * N_EXPERTS):
            @pl.when(zrow_ref[j] >= 0)
            def _():
                pltpu.make_async_copy(zero_buf, xs_ref.at[pl.ds(0, tb * SUBLANES)], sem_zero).wait()
        idx_copy(0, 0).start()
        for j in range(min(STAGE_RING - 1, n_tiles)):
            load(j).start()

    cur = i % 2
    idx_copy(i, cur).wait()

    @pl.when(i + 1 < n_tiles)
    def _():
        idx_copy(i + 1, 1 - cur).start()

    load(i).wait()
    src_tile = stage.at[i % STAGE_RING]
    sem = sem_row.at[i % STAGE_RING]
    bases = [cur * n + k * tm for k in range(TOP_K)]

    def body(t, c):
        src = src_tile.at[pl.ds(pl.multiple_of(t * SUBLANES, SUBLANES), SUBLANES)]
        for k in range(TOP_K):
            row = pl.multiple_of(slot_smem[bases[k] + t], SUBLANES)
            pltpu.make_async_copy(src, xs_ref.at[pl.ds(row, SUBLANES)], sem).start(priority=k % 2)
        return c

    lax.fori_loop(0, tm, body, 0, unroll=ISSUE_GROUP)

    @pl.when(i > 0)
    def _():
        drain(i - 1)

    @pl.when(i + STAGE_RING - 1 < n_tiles)
    def _():
        load(i + STAGE_RING - 1).start()

    @pl.when(i == n_tiles - 1)
    def _():
        drain(i)


def _dispatch(zrow, dest_tiles, h2_tiles, n_slots, tm, tb):
    t = h2_tiles.shape[0] // SUBLANES
    n_tiles = t // tm
    grid_spec = pltpu.PrefetchScalarGridSpec(
        num_scalar_prefetch=1,
        grid=(n_tiles,),
        in_specs=[pl.BlockSpec(memory_space=pl.ANY), pl.BlockSpec(memory_space=pl.ANY)],
        out_specs=pl.BlockSpec(memory_space=pl.ANY),
        scratch_shapes=[pltpu.SMEM((2 * tm * TOP_K,), I32), pltpu.VMEM((tb * SUBLANES, LANES), F32),
                        pltpu.VMEM((STAGE_RING, tm * SUBLANES, LANES), F32),
                        pltpu.SemaphoreType.DMA((2,)), pltpu.SemaphoreType.DMA((STAGE_RING,)),
                        pltpu.SemaphoreType.DMA((STAGE_RING,)), pltpu.SemaphoreType.DMA],
    )
    return pl.pallas_call(
        functools.partial(_dispatch_kernel, tm=tm, tb=tb, n_tiles=n_tiles),
        grid_spec=grid_spec,
        out_shape=jax.ShapeDtypeStruct((n_slots * SUBLANES, LANES), F32),
        compiler_params=_cparams("arbitrary"),
        name="dispatch",
    )(zrow, dest_tiles, h2_tiles)


def _expert_kernel(be_ref, bx_ref, bv_ref, xs_ref, wgu_ref, bgu_ref, wd_ref, bd_ref, o_ref, wgu_bf, wd_bf,
                   *, tb, dff):
    i = pl.program_id(0)

    @pl.when(jnp.logical_or(i == 0, be_ref[i] != be_ref[jnp.maximum(i - 1, 0)]))
    def _():
        wgu_bf[...] = wgu_ref[0].astype(BF16)
        wd_bf[...] = wd_ref[0].astype(BF16)

    @pl.when(bv_ref[i] > 0)
    def _():
        x = _load_token_tiles(xs_ref, tb).astype(BF16)
        hu = jnp.dot(x, wgu_bf[...], preferred_element_type=F32) + bgu_ref[0]
        a = jnp.minimum(hu[:, :dff], SWIGLU_LIMIT)
        lin = jnp.clip(hu[:, dff:], -SWIGLU_LIMIT, SWIGLU_LIMIT)
        act = a * (1.0 / (1.0 + jnp.exp(-SWIGLU_ALPHA * a))) * (lin + 1.0)
        out = jnp.dot(act.astype(BF16), wd_bf[...], preferred_element_type=F32) + bd_ref[0]
        _store_token_tiles(o_ref, out, tb)

    @pl.when(bv_ref[i] == 0)
    def _():
        o_ref[...] = jnp.zeros_like(o_ref)


def _experts(blk_e, blk_x, blk_v, xs_tiles, wgu, bgu, wd, bd, tb):
    d, dff = wd.shape[2], wd.shape[1]
    n_slots = xs_tiles.shape[0] // SUBLANES
    nb = n_slots // tb
    rows = tb * SUBLANES
    grid_spec = pltpu.PrefetchScalarGridSpec(
        num_scalar_prefetch=3,
        grid=(nb,),
        in_specs=[pl.BlockSpec((rows, LANES), lambda i, be, bx, bv: (bx[i], 0)),
                  pl.BlockSpec((1, d, 2 * dff), lambda i, be, bx, bv: (be[i], 0, 0)),
                  pl.BlockSpec((1, 1, 2 * dff), lambda i, be, bx, bv: (be[i], 0, 0)),
                  pl.BlockSpec((1, dff, d), lambda i, be, bx, bv: (be[i], 0, 0)),
                  pl.BlockSpec((1, 1, d), lambda i, be, bx, bv: (be[i], 0, 0))],
        out_specs=pl.BlockSpec((rows, LANES), lambda i, be, bx, bv: (i, 0)),
        scratch_shapes=[pltpu.VMEM((d, 2 * dff), BF16), pltpu.VMEM((dff, d), BF16)],
    )
    return pl.pallas_call(
        functools.partial(_expert_kernel, tb=tb, dff=dff),
        grid_spec=grid_spec,
        out_shape=jax.ShapeDtypeStruct((n_slots * SUBLANES, LANES), F32),
        compiler_params=_cparams("arbitrary"),
        name="experts",
    )(blk_e, blk_x, blk_v, xs_tiles, wgu, bgu, wd, bd)


def _combine_kernel(dest_ref, x1_ref, rg_ref, fg_ref, mo_ref, out_ref, slot_smem, gbuf, sem_idx, sem_row,
                    *, tm, n_tiles):
    i = pl.program_id(0)
    n = tm * TOP_K

    def idx_copy(tile):
        buf = tile % IDX_RING
        return pltpu.make_async_copy(dest_ref.at[pl.ds(tile * n, n)],
                                     slot_smem.at[pl.ds(pl.multiple_of(buf * n, n), n)], sem_idx.at[buf])

    def issue(tile):
        gb = tile % GATHER_RING
        bases = [(tile % IDX_RING) * n + k * tm for k in range(TOP_K)]
        dsts = [gbuf.at[gb, k] for k in range(TOP_K)]
        sem = sem_row.at[gb]

        def body(t, c):
            dst = pl.ds(pl.multiple_of(t * SUBLANES, SUBLANES), SUBLANES)
            for k in range(TOP_K):
                row = pl.multiple_of(slot_smem[bases[k] + t], SUBLANES)
                pltpu.make_async_copy(mo_ref.at[pl.ds(row, SUBLANES)], dsts[k].at[dst], sem).start(priority=k % 2)
            return c

        lax.fori_loop(0, tm, body, 0, unroll=ISSUE_GROUP)

    @pl.when(i == 0)
    def _():
        idx_copy(0).start()
        if n_tiles > 1:
            idx_copy(1).start()
        idx_copy(0).wait()
        issue(0)

    @pl.when(i + 1 < n_tiles)
    def _():
        idx_copy(i + 1).wait()
        issue(i + 1)

    @pl.when(i + 2 < n_tiles)
    def _():
        idx_copy(i + 2).start()

    cur = i % GATHER_RING
    for k in range(TOP_K):
        pltpu.make_async_copy(mo_ref.at[pl.ds(0, tm * SUBLANES)], gbuf.at[cur, k], sem_row.at[cur]).wait()

    g_rows = jnp.concatenate([rg_ref[...], jnp.zeros((LANES - ROUTE_ROWS, tm), F32)], axis=0)
    g = g_rows.T
    y = x1_ref[...]
    for k in range(TOP_K):
        y = y + g[:, k:k + 1] * _load_token_tiles(gbuf.at[cur, k], tm)
    out_ref[...] = _rms(y, fg_ref[...])


def _combine(dest_tiles, x1, rg, fg, mlp_out, tm):
    t, d = x1.shape
    n_tiles = t // tm
    rowd = pl.BlockSpec((tm, d), lambda i: (i, 0))
    return pl.pallas_call(
        functools.partial(_combine_kernel, tm=tm, n_tiles=n_tiles),
        grid=(n_tiles,),
        in_specs=[pl.BlockSpec(memory_space=pl.ANY), rowd,
                  pl.BlockSpec((ROUTE_ROWS, tm), lambda i: (0, i)),
                  pl.BlockSpec((1, d), lambda i: (0, 0)),
                  pl.BlockSpec(memory_space=pl.ANY)],
        out_specs=rowd,
        out_shape=jax.ShapeDtypeStruct((t, d), F32),
        scratch_shapes=[pltpu.SMEM((IDX_RING * tm * TOP_K,), I32),
                        pltpu.VMEM((GATHER_RING, TOP_K, tm * SUBLANES, LANES), F32),
                        pltpu.SemaphoreType.DMA((IDX_RING,)), pltpu.SemaphoreType.DMA((GATHER_RING,))],
        compiler_params=_cparams("arbitrary"),
        name="combine",
    )(dest_tiles, x1, rg, fg, mlp_out)


def _block_tables(counts, idx, rank, tb, nb):
    padded = (counts + tb - 1) // tb * tb
    pad_end = jnp.cumsum(padded)
    pad_start = pad_end - padded
    sel = idx[..., None] == jnp.arange(N_EXPERTS, dtype=I32)
    dest = jnp.sum(jnp.where(sel, pad_start, 0), axis=-1) + rank
    blocks_e = padded // tb
    blk_end = jnp.cumsum(blocks_e)
    total = blk_end[-1]
    bid = jnp.arange(nb, dtype=I32)
    src = jnp.minimum(bid, total - 1)
    e = jnp.minimum(jnp.sum((src[:, None] >= blk_end[None, :]).astype(I32), axis=1), N_EXPERTS - 1)
    first = jnp.sum(jnp.where(e[:, None] == jnp.arange(N_EXPERTS), blk_end - blocks_e, 0), axis=1)
    cnt = jnp.sum(jnp.where(e[:, None] == jnp.arange(N_EXPERTS), counts, 0), axis=1)
    valid = jnp.where(bid < total, jnp.clip(cnt - (src - first) * tb, 0, tb), 0)
    tail = total + jnp.arange(N_EXPERTS, dtype=I32)
    zrow = jnp.concatenate([jnp.where(padded > 0, pad_end - tb, -1), jnp.where(tail < nb, tail * tb, -1)])
    return dest.astype(I32), e.astype(I32), src.astype(I32), valid.astype(I32), zrow.astype(I32)


def _tile_major(dest, tm):
    k, t = dest.shape
    return (dest * SUBLANES).reshape(k, t // tm, tm).transpose(1, 0, 2).reshape(-1)


def kernel(x, mix_norm_g, w_in, conv_w, attn_out_norm_g, conv_out_norm_g, w_out, ffn_norm_g, w_router,
           b_router, w_gate_up, b_gate_up, w_down, b_down, final_norm_g):
    batch, seq, d = x.shape
    t = batch * seq
    assert w_in.shape[0] == 1, "single-layer trunk: the final RMSNorm is fused into the combine kernel"
    assert d == SUBLANES * LANES, "one token row must fill exactly one (8, 128) f32 tile"
    tm = 512
    tb = 512
    tc = 512
    td = 1024
    nb = (t * TOP_K) // tb + N_EXPERTS
    x2 = x.reshape(t, d)

    *qkv, gb, z = _proj(x2, mix_norm_g[0][None, :], w_in[0].astype(BF16), tm, batch, seq)
    o_views, l_views = [], []
    for di, dil in enumerate(DILATIONS):
        o, lse = _attn_branch(qkv[di], qkv[3 + di], qkv[6 + di], dil)
        o_views.append(o)
        l_views.append(lse)

    x1, h2, ri, rg, cnt = _mix(x2, o_views, l_views, gb, z, conv_w[0], attn_out_norm_g[0][None, :],
                               conv_out_norm_g[0][None, :], w_out[0].astype(BF16), ffn_norm_g[0][None, :],
                               w_router[0], b_router[0], tm, batch, seq)

    counts = cnt[:, 0].astype(I32)
    dest, blk_e, blk_x, blk_v, zrow = _block_tables(counts, ri[:TOP_K], ri[TOP_K:], tb, nb)

    xs = _dispatch(zrow, _tile_major(dest, td), h2, nb * tb, td, tb)
    mo = _experts(blk_e, blk_x, blk_v, xs, w_gate_up[0], b_gate_up[0][:, None, :],
                  w_down[0], b_down[0][:, None, :], tb)
    out = _combine(_tile_major(dest, tc), x1, rg, final_norm_g[None, :], mo, tc)
    return out.reshape(batch, seq, d)
```

```python
import functools

import numpy as np
import jax
import jax.numpy as jnp
from jax import lax
from jax.experimental import pallas as pl
from jax.experimental.pallas import tpu as pltpu

F32 = jnp.float32
BF16 = jnp.bfloat16
I32 = jnp.int32

HEAD_DIM = 64
N_HEADS = 8
ATTN_W = N_HEADS * HEAD_DIM
CONV_W = 512
N_EXPERTS = 32
TOP_K = 4
DILATIONS = (1, 4, 16)
SIDE = 64
SWIGLU_ALPHA = 1.702
SWIGLU_LIMIT = 7.0
RMS_EPS = 1e-5
NEG = -1e30
LOG2E = 1.4426950408889634

LANES = 128
SLABS = ATTN_W // LANES
Q_SUB = 128
K_WIN = Q_SUB + 2 * SIDE
VMEM_LIMIT = 56 * 1024 * 1024


def _cparams(*sem):
    return pltpu.CompilerParams(dimension_semantics=sem, vmem_limit_bytes=VMEM_LIMIT)


def _rms(x, g):
    return x * lax.rsqrt(jnp.mean(x * x, axis=-1, keepdims=True) + RMS_EPS) * g


SUBLANES = 8
BF16_ROWS = 16
STAT_LANES = LANES // N_HEADS
ROUTE_ROWS = 2 * TOP_K
ISSUE_GROUP = 8
STAGE_RING = 3
GATHER_RING = 2
IDX_RING = GATHER_RING + 1


def _store_token_tiles(ref, x, n):
    for s in range(x.shape[1] // LANES):
        ref[pl.ds(s, n, stride=SUBLANES), :] = x[:, s * LANES:(s + 1) * LANES]


def _load_token_tiles(ref, n):
    return jnp.concatenate([ref[pl.ds(s, n, stride=SUBLANES), :] for s in range(SUBLANES)], axis=1)


def _split_bf16(a):
    hi = a.astype(BF16)
    return hi, (a - hi.astype(F32)).astype(BF16)


def _proj_kernel(x_ref, g_ref, w_ref, *refs, tm):
    outs = refs[:9]
    gb_ref, z_ref, pbuf = refs[9:]
    h = _rms(x_ref[...], g_ref[...]).astype(BF16)

    def proj(j):
        return jnp.dot(h, w_ref[:, j * ATTN_W:(j + 1) * ATTN_W], preferred_element_type=F32)

    for j in range(3):
        p = proj(j)
        if j == 0:
            p = p * (LOG2E * HEAD_DIM ** -0.5)
        outs[3 * j][0] = p.astype(BF16)
        for c in range(SLABS):
            pbuf[j, c] = p[:, c * LANES:(c + 1) * LANES]
        for di in (1, 2):
            dil = DILATIONS[di]
            o = outs[3 * j + di]
            for r in range(dil):
                for c in range(SLABS):
                    lo = r * ATTN_W + c * LANES
                    o[0, :, lo:lo + LANES] = pbuf[j, c, pl.ds(r, tm // dil, stride=dil), :].astype(BF16)
    gb_ref[...] = proj(3).astype(BF16)
    z_ref[...] = (proj(4) * proj(5)).astype(BF16)


def _proj(x2, g, w_in_bf, tm, batch, seq):
    t, d = x2.shape
    tps = seq // tm
    row = pl.BlockSpec((tm, ATTN_W), lambda i: (i, 0))
    view_specs, view_shapes = [], []
    for _ in range(3):
        for dil in DILATIONS:
            view_specs.append(pl.BlockSpec((1, tm // dil, dil * ATTN_W), lambda i: (i // tps, i % tps, 0)))
            view_shapes.append(jax.ShapeDtypeStruct((batch, seq // dil, dil * ATTN_W), BF16))
    flat = jax.ShapeDtypeStruct((t, ATTN_W), BF16)
    return pl.pallas_call(
        functools.partial(_proj_kernel, tm=tm),
        grid=(t // tm,),
        in_specs=[pl.BlockSpec((tm, d), lambda i: (i, 0)),
                  pl.BlockSpec((1, d), lambda i: (0, 0)),
                  pl.BlockSpec(w_in_bf.shape, lambda i: (0, 0))],
        out_specs=view_specs + [row, row],
        out_shape=view_shapes + [flat, flat],
        scratch_shapes=[pltpu.VMEM((3, SLABS, tm, LANES), F32)],
        compiler_params=_cparams("parallel"),
        name="proj",
    )(x2, g, w_in_bf)


def _bias_tables(dil):
    row = np.arange(Q_SUB)[:, None]
    col = np.arange(K_WIN)[None, :]
    rel = col - SIDE - row
    band = np.abs(rel) <= SIDE
    slopes = 2.0 ** (-8.0 * np.arange(1, N_HEADS + 1) / N_HEADS)
    tabs = []
    for t in range(4):
        ok = band.copy()
        if t & 1:
            ok &= col >= SIDE
        if t & 2:
            ok &= col < SIDE + Q_SUB
        for h in range(N_HEADS):
            tabs.append(np.where(ok, -slopes[h] * np.abs(rel) * dil * LOG2E, NEG))
    return jnp.asarray(np.stack(tabs), dtype=F32)


def _attn_kernel(q_ref, kp_ref, km_ref, kn_ref, vp_ref, vm_ref, vn_ref, bias_ref,
                 o_ref, st_ref, kbuf, vbuf, *, tq, n_sub_total):
    i = pl.program_id(2)
    pieces = ((0, SIDE, kp_ref, vp_ref), (SIDE, SIDE + tq, km_ref, vm_ref), (SIDE + tq, 2 * SIDE + tq, kn_ref, vn_ref))
    for lo_row, hi_row, k_ref, v_ref in pieces:
        kbuf[lo_row:hi_row] = k_ref[0]
        for hp in range(N_HEADS // 2):
            vbuf[lo_row:hi_row, 2 * hp * LANES:(2 * hp + 1) * LANES] = v_ref[0, :, hp * LANES:(hp + 1) * LANES]

    @pl.when((pl.program_id(0) == 0) & (pl.program_id(1) == 0) & (i == 0))
    def _():
        for hp in range(N_HEADS // 2):
            vbuf[:, (2 * hp + 1) * LANES:(2 * hp + 2) * LANES] = jnp.ones((tq + 2 * SIDE, LANES), BF16)

    lane = lax.broadcasted_iota(I32, (Q_SUB, LANES), 1)
    lo = lane < HEAD_DIM
    grp = lane // STAT_LANES
    n_sub = tq // Q_SUB

    def sub(j, carry):
        r0 = pl.multiple_of(j * Q_SUB, Q_SUB)
        g_sub = i * n_sub + j
        tbl = (g_sub == 0).astype(I32) + 2 * (g_sub == n_sub_total - 1).astype(I32)
        m_tile = jnp.zeros((Q_SUB, LANES), F32)
        l_tile = jnp.zeros((Q_SUB, LANES), F32)
        for hp in range(N_HEADS // 2):
            cols = slice(hp * LANES, (hp + 1) * LANES)
            qp = q_ref[0, pl.ds(r0, Q_SUB), cols]
            kw = kbuf[pl.ds(r0, K_WIN), cols]
            vw = vbuf[pl.ds(r0, K_WIN), 2 * hp * LANES:(2 * hp + 2) * LANES]
            outs = []
            for par in range(2):
                qm = jnp.where(lo if par == 0 else jnp.logical_not(lo), qp, jnp.zeros_like(qp))
                s = lax.dot_general(qm, kw, (((1,), (1,)), ((), ())), preferred_element_type=F32)
                s = s + bias_ref[tbl * N_HEADS + 2 * hp + par]
                m = jnp.max(s, axis=1, keepdims=True)
                p = jnp.exp2(s - m).astype(BF16)
                pvl = jnp.dot(p, vw, preferred_element_type=F32)
                outs.append(pvl[:, :LANES])
                m_tile = jnp.where(grp == 2 * hp + par, m, m_tile)
                l_tile = jnp.where(grp == 2 * hp + par, pvl[:, LANES:], l_tile)
            o_ref[0, pl.ds(r0, Q_SUB), cols] = jnp.where(lo, outs[0], outs[1]).astype(BF16)
        st_ref[0, pl.ds(r0, Q_SUB), 0:LANES] = m_tile
        st_ref[0, pl.ds(r0, Q_SUB), LANES:2 * LANES] = l_tile
        return carry

    lax.fori_loop(0, n_sub, sub, 0, unroll=True)


def _attn_branch(qv, kv, vv, dil):
    batch, sd, _ = qv.shape
    tq = min(1024, sd)
    n_tiles = sd // tq
    bias = _bias_tables(dil)
    r_side = tq // SIDE
    last_side = sd // SIDE - 1

    main = pl.BlockSpec((1, tq, ATTN_W), lambda b, r, i: (b, i, r))
    prev = pl.BlockSpec((1, SIDE, ATTN_W), lambda b, r, i: (b, jnp.maximum(i * r_side - 1, 0), r))
    nxt = pl.BlockSpec((1, SIDE, ATTN_W), lambda b, r, i: (b, jnp.minimum((i + 1) * r_side, last_side), r))
    st_spec = pl.BlockSpec((1, tq, 2 * LANES), lambda b, r, i: (b, i, r))
    return pl.pallas_call(
        functools.partial(_attn_kernel, tq=tq, n_sub_total=sd // Q_SUB),
        grid=(batch, dil, n_tiles),
        in_specs=[main, prev, main, nxt, prev, main, nxt,
                  pl.BlockSpec(bias.shape, lambda b, r, i: (0, 0, 0))],
        out_specs=[main, st_spec],
        out_shape=[jax.ShapeDtypeStruct((batch, sd, dil * ATTN_W), BF16),
                   jax.ShapeDtypeStruct((batch, sd, dil * 2 * LANES), F32)],
        scratch_shapes=[pltpu.VMEM((tq + 2 * SIDE, ATTN_W), BF16),
                        pltpu.VMEM((tq + 2 * SIDE, 2 * ATTN_W), BF16)],
        compiler_params=_cparams("arbitrary", "arbitrary", "arbitrary"),
        name=f"attn_d{dil}",
    )(qv, kv, kv, kv, vv, vv, vv, bias)


def _mix_kernel(x_ref, o1_ref, o4_ref, o16_ref, l1_ref, l4_ref, l16_ref, gb_ref, z_ref, zp_ref, zn_ref,
                cw_ref, ga_ref, gc_ref, wo_ref, gf_ref, wr2_ref, br_ref, upper_ref, expand_ref,
                x1_ref, h2_ref, ri_ref, rg_ref, cnt_ref, onat, lnat, carry, *, tm, tiles_per_seq):
    i = pl.program_id(0)

    @pl.when(i == 0)
    def _():
        carry[...] = jnp.zeros_like(carry)

    for bi, (o_ref, l_ref, dil) in enumerate(((o4_ref, l4_ref, 4), (o16_ref, l16_ref, 16))):
        n = tm // dil
        for r in range(dil):
            for c in range(2):
                lo = (2 * r + c) * LANES
                lnat[bi, c, pl.ds(r, n, stride=dil), :] = l_ref[0, :, lo:lo + LANES]
            for c in range(SLABS):
                lo = r * ATTN_W + c * LANES
                onat[bi, c, pl.ds(r, n, stride=dil), :] = o_ref[0, :, lo:lo + LANES].astype(F32)

    ms = [l1_ref[0, :, 0:LANES], lnat[0, 0], lnat[1, 0]]
    ls = [l1_ref[0, :, LANES:2 * LANES], lnat[0, 1], lnat[1, 1]]
    mx = jnp.maximum(jnp.maximum(ms[0], ms[1]), ms[2])
    w = [jnp.exp2(t - mx) for t in ms]
    inv = 1.0 / (w[0] * ls[0] + w[1] * ls[1] + w[2] * ls[2])
    wide = []
    for g in range(3):
        hi_lo = jnp.concatenate(_split_bf16(w[g] * inv), axis=1)
        wide.append(jnp.dot(hi_lo, expand_ref[...], preferred_element_type=F32))
    slabs = []
    for c in range(SLABS):
        cs = slice(c * LANES, (c + 1) * LANES)
        slabs.append(wide[0][:, cs] * o1_ref[0, :, cs].astype(F32)
                     + wide[1][:, cs] * onat[0, c] + wide[2][:, cs] * onat[1, c])
    attn = jnp.concatenate(slabs, axis=1)

    z = z_ref[...].astype(F32)
    row = lax.broadcasted_iota(I32, (tm, 1), 0)
    seq_first = (i % tiles_per_seq) == 0
    seq_last = (i % tiles_per_seq) == tiles_per_seq - 1
    z_before = jnp.where(seq_first, 0.0, zp_ref[BF16_ROWS - 1:BF16_ROWS, :].astype(F32))
    z_after = jnp.where(seq_last, 0.0, zn_ref[0:1, :].astype(F32))
    z_up = jnp.where(row == 0, z_before, pltpu.roll(z, 1, 0))
    z_dn = jnp.where(row == tm - 1, z_after, pltpu.roll(z, tm - 1, 0))
    conv = gb_ref[...].astype(F32) * (cw_ref[0:1, :] * z_up + cw_ref[1:2, :] * z + cw_ref[2:3, :] * z_dn)

    na = _rms(attn, ga_ref[...]).astype(BF16)
    nc = _rms(conv, gc_ref[...]).astype(BF16)
    mix = (jnp.dot(na, wo_ref[0:ATTN_W, :], preferred_element_type=F32)
           + jnp.dot(nc, wo_ref[ATTN_W:, :], preferred_element_type=F32))
    x1 = x_ref[...] + mix
    x1_ref[...] = x1
    h2 = _rms(x1, gf_ref[...])
    _store_token_tiles(h2_ref, h2, tm)

    hh, hl = _split_bf16(h2)
    both = jnp.dot(hh, wr2_ref[...], preferred_element_type=F32)
    logits = (both[:, :LANES] + both[:, LANES:]
              + jnp.dot(hl, wr2_ref[:, :LANES], preferred_element_type=F32) + br_ref[...])
    lt = logits.T[0:N_EXPERTS, :]

    erow = lax.broadcasted_iota(I32, (N_EXPERTS, tm), 0)
    work = lt
    idx, val = [], []
    for _ in range(TOP_K):
        mk = jnp.max(work, axis=0, keepdims=True)
        ik = jnp.min(jnp.where(work == mk, erow, N_EXPERTS), axis=0, keepdims=True)
        idx.append(ik)
        val.append(mk)
        work = jnp.where(erow == ik, 2 * NEG, work)
    ex = [jnp.exp(v - val[0]) for v in val]
    ginv = 1.0 / (ex[0] + ex[1] + ex[2] + ex[3])
    gates = [e * ginv for e in ex]

    hit = [erow == ik for ik in idx]
    onehot = (hit[0] | hit[1] | hit[2] | hit[3]).astype(F32)
    before = jnp.dot(onehot.astype(BF16), upper_ref[...], preferred_element_type=F32) + carry[:, 0:1]
    ranks = [jnp.sum(jnp.where(h, before, 0.0), axis=0, keepdims=True) for h in hit]
    new_carry = carry[:, 0:1] + jnp.sum(onehot, axis=1, keepdims=True)
    carry[...] = jnp.broadcast_to(new_carry, carry.shape)
    cnt_ref[...] = jnp.broadcast_to(new_carry, cnt_ref.shape)

    ri_ref[...] = jnp.concatenate(idx + [r.astype(I32) for r in ranks], axis=0)
    rg_ref[...] = jnp.concatenate(gates + [jnp.zeros((TOP_K, tm), F32)], axis=0)


def _mix(x2, o_views, l_views, gb, z, conv_w, ga, gc, w_out_bf, gf, w_router, b_router, tm, batch, seq):
    t, d = x2.shape
    n = t // tm
    tps = seq // tm
    upper = jnp.asarray(np.triu(np.ones((tm, tm), np.float32), 1), dtype=BF16)
    sel = np.arange(LANES)[:, None] == STAT_LANES * (np.arange(ATTN_W)[None, :] // HEAD_DIM)
    expand = jnp.asarray(np.concatenate([sel, sel], axis=0), dtype=BF16)
    wr_pad = jnp.zeros((d, LANES), F32).at[:, :N_EXPERTS].set(w_router)
    wr2 = jnp.concatenate(_split_bf16(wr_pad), axis=1)
    br_pad = jnp.zeros((1, LANES), F32).at[0, :N_EXPERTS].set(b_router)

    rowd = pl.BlockSpec((tm, d), lambda i: (i, 0))
    rowa = pl.BlockSpec((tm, ATTN_W), lambda i: (i, 0))
    halo = tm // BF16_ROWS
    zprev = pl.BlockSpec((BF16_ROWS, CONV_W), lambda i: (jnp.maximum(i * halo - 1, 0), 0))
    znext = pl.BlockSpec((BF16_ROWS, CONV_W), lambda i: (jnp.minimum((i + 1) * halo, t // BF16_ROWS - 1), 0))

    def view(width, dil):
        return pl.BlockSpec((1, tm // dil, dil * width), lambda i: (i // tps, i % tps, 0))

    def full(a):
        return pl.BlockSpec(a.shape, lambda i: (0,) * a.ndim)

    consts = (conv_w, ga, gc, w_out_bf, gf, wr2, br_pad, upper, expand)
    return pl.pallas_call(
        functools.partial(_mix_kernel, tm=tm, tiles_per_seq=tps),
        grid=(n,),
        in_specs=[rowd] + [view(ATTN_W, dil) for dil in DILATIONS] + [view(2 * LANES, dil) for dil in DILATIONS]
                 + [rowa, rowa, zprev, znext] + [full(a) for a in consts],
        out_specs=[rowd, pl.BlockSpec((tm * SUBLANES, LANES), lambda i: (i, 0)),
                   pl.BlockSpec((ROUTE_ROWS, tm), lambda i: (0, i)), pl.BlockSpec((ROUTE_ROWS, tm), lambda i: (0, i)),
                   pl.BlockSpec((N_EXPERTS, LANES), lambda i: (0, 0))],
        out_shape=[jax.ShapeDtypeStruct((t, d), F32), jax.ShapeDtypeStruct((t * SUBLANES, LANES), F32),
                   jax.ShapeDtypeStruct((ROUTE_ROWS, t), I32), jax.ShapeDtypeStruct((ROUTE_ROWS, t), F32),
                   jax.ShapeDtypeStruct((N_EXPERTS, LANES), F32)],
        scratch_shapes=[pltpu.VMEM((2, SLABS, tm, LANES), F32), pltpu.VMEM((2, 2, tm, LANES), F32),
                        pltpu.VMEM((N_EXPERTS, LANES), F32)],
        compiler_params=_cparams("arbitrary"),
        name="mix_router",
    )(x2, *o_views, *l_views, gb, z, z, z, *consts)


def _dispatch_kernel(zrow_ref, dest_ref, h_ref, xs_ref, slot_smem, zero_buf, stage, sem_idx, sem_in, sem_row,
                     sem_zero, *, tm, tb, n_tiles):
    i = pl.program_id(0)
    n = tm * TOP_K
    rows = tm * SUBLANES

    def idx_copy(tile, buf):
        return pltpu.make_async_copy(dest_ref.at[pl.ds(tile * n, n)],
                                     slot_smem.at[pl.ds(pl.multiple_of(buf * n, n), n)], sem_idx.at[buf])

    def load(tile):
        buf = tile % STAGE_RING
        return pltpu.make_async_copy(h_ref.at[pl.ds(pl.multiple_of(tile * rows, rows), rows)], stage.at[buf],
                                     sem_in.at[buf])

    def drain(tile):
        buf = tile % STAGE_RING
        for _ in range(TOP_K):
            pltpu.make_async_copy(stage.at[buf], xs_ref.at[pl.ds(0, rows)], sem_row.at[buf]).wait()

    @pl.when(i == 0)
    def _():
        zero_buf[...] = jnp.zeros_like(zero_buf)
        for j in range(2 * N_EXPERTS):
            @pl.when(zrow_ref[j] >= 0)
            def _():
                start = pl.multiple_of(zrow_ref[j] * SUBLANES, tb * SUBLANES)
                pltpu.make_async_copy(zero_buf, xs_ref.at[pl.ds(start, tb * SUBLANES)], sem_zero).start()
        for j in range(2 * N_EXPERTS):
            @pl.when(zrow_ref[j] >= 0)
            def _():
                pltpu.make_async_copy(zero_buf, xs_ref.at[pl.ds(0, tb * SUBLANES)], sem_zero).wait()
        idx_copy(0, 0).start()
        for j in range(min(STAGE_RING - 1, n_tiles)):
            load(j).start()

    cur = i % 2
    idx_copy(i, cur).wait()

    @pl.when(i + 1 < n_tiles)
    def _():
        idx_copy(i + 1, 1 - cur).start()

    load(i).wait()
    src_tile = stage.at[i % STAGE_RING]
    sem = sem_row.at[i % STAGE_RING]
    bases = [cur * n + k * tm for k in range(TOP_K)]

    def body(t, c):
        src = src_tile.at[pl.ds(pl.multiple_of(t * SUBLANES, SUBLANES), SUBLANES)]
        for k in range(TOP_K):
            row = pl.multiple_of(slot_smem[bases[k] + t], SUBLANES)
            pltpu.make_async_copy(src, xs_ref.at[pl.ds(row, SUBLANES)], sem).start(priority=k % 2)
        return c

    lax.fori_loop(0, tm, body, 0, unroll=ISSUE_GROUP)

    @pl.when(i > 0)
    def _():
        drain(i - 1)

    @pl.when(i + STAGE_RING - 1 < n_tiles)
    def _():
        load(i + STAGE_RING - 1).start()

    @pl.when(i == n_tiles - 1)
    def _():
        drain(i)


def _dispatch(zrow, dest_tiles, h2_tiles, n_slots, tm, tb):
    t = h2_tiles.shape[0] // SUBLANES
    n_tiles = t // tm
    grid_spec = pltpu.PrefetchScalarGridSpec(
        num_scalar_prefetch=1,
        grid=(n_tiles,),
        in_specs=[pl.BlockSpec(memory_space=pl.ANY), pl.BlockSpec(memory_space=pl.ANY)],
        out_specs=pl.BlockSpec(memory_space=pl.ANY),
        scratch_shapes=[pltpu.SMEM((2 * tm * TOP_K,), I32), pltpu.VMEM((tb * SUBLANES, LANES), F32),
                        pltpu.VMEM((STAGE_RING, tm * SUBLANES, LANES), F32),
                        pltpu.SemaphoreType.DMA((2,)), pltpu.SemaphoreType.DMA((STAGE_RING,)),
                        pltpu.SemaphoreType.DMA((STAGE_RING,)), pltpu.SemaphoreType.DMA],
    )
    return pl.pallas_call(
        functools.partial(_dispatch_kernel, tm=tm, tb=tb, n_tiles=n_tiles),
        grid_spec=grid_spec,
        out_shape=jax.ShapeDtypeStruct((n_slots * SUBLANES, LANES), F32),
        compiler_params=_cparams("arbitrary"),
        name="dispatch",
    )(zrow, dest_tiles, h2_tiles)


def _expert_kernel(be_ref, bx_ref, bv_ref, xs_ref, wgu_ref, bgu_ref, wd_ref, bd_ref, o_ref, wgu_bf, wd_bf,
                   *, tb, dff):
    i = pl.program_id(0)

    @pl.when(jnp.logical_or(i == 0, be_ref[i] != be_ref[jnp.maximum(i - 1, 0)]))
    def _():
        wgu_bf[...] = wgu_ref[0].astype(BF16)
        wd_bf[...] = wd_ref[0].astype(BF16)

    @pl.when(bv_ref[i] > 0)
    def _():
        x = _load_token_tiles(xs_ref, tb).astype(BF16)
        hu = jnp.dot(x, wgu_bf[...], preferred_element_type=F32) + bgu_ref[0]
        a = jnp.minimum(hu[:, :dff], SWIGLU_LIMIT)
        lin = jnp.clip(hu[:, dff:], -SWIGLU_LIMIT, SWIGLU_LIMIT)
        act = a * (1.0 / (1.0 + jnp.exp(-SWIGLU_ALPHA * a))) * (lin + 1.0)
        out = jnp.dot(act.astype(BF16), wd_bf[...], preferred_element_type=F32) + bd_ref[0]
        _store_token_tiles(o_ref, out, tb)

    @pl.when(bv_ref[i] == 0)
    def _():
        o_ref[...] = jnp.zeros_like(o_ref)


def _experts(blk_e, blk_x, blk_v, xs_tiles, wgu, bgu, wd, bd, tb):
    d, dff = wd.shape[2], wd.shape[1]
    n_slots = xs_tiles.shape[0] // SUBLANES
    nb = n_slots // tb
    rows = tb * SUBLANES
    grid_spec = pltpu.PrefetchScalarGridSpec(
        num_scalar_prefetch=3,
        grid=(nb,),
        in_specs=[pl.BlockSpec((rows, LANES), lambda i, be, bx, bv: (bx[i], 0)),
                  pl.BlockSpec((1, d, 2 * dff), lambda i, be, bx, bv: (be[i], 0, 0)),
                  pl.BlockSpec((1, 1, 2 * dff), lambda i, be, bx, bv: (be[i], 0, 0)),
                  pl.BlockSpec((1, dff, d), lambda i, be, bx, bv: (be[i], 0, 0)),
                  pl.BlockSpec((1, 1, d), lambda i, be, bx, bv: (be[i], 0, 0))],
        out_specs=pl.BlockSpec((rows, LANES), lambda i, be, bx, bv: (i, 0)),
        scratch_shapes=[pltpu.VMEM((d, 2 * dff), BF16), pltpu.VMEM((dff, d), BF16)],
    )
    return pl.pallas_call(
        functools.partial(_expert_kernel, tb=tb, dff=dff),
        grid_spec=grid_spec,
        out_shape=jax.ShapeDtypeStruct((n_slots * SUBLANES, LANES), F32),
        compiler_params=_cparams("arbitrary"),
        name="experts",
    )(blk_e, blk_x, blk_v, xs_tiles, wgu, bgu, wd, bd)


def _combine_kernel(dest_ref, x1_ref, rg_ref, fg_ref, mo_ref, out_ref, slot_smem, gbuf, sem_idx, sem_row,
                    *, tm, n_tiles):
    i = pl.program_id(0)
    n = tm * TOP_K

    def idx_copy(tile):
        buf = tile % IDX_RING
        return pltpu.make_async_copy(dest_ref.at[pl.ds(tile * n, n)],
                                     slot_smem.at[pl.ds(pl.multiple_of(buf * n, n), n)], sem_idx.at[buf])

    def issue(tile):
        gb = tile % GATHER_RING
        bases = [(tile % IDX_RING) * n + k * tm for k in range(TOP_K)]
        dsts = [gbuf.at[gb, k] for k in range(TOP_K)]
        sem = sem_row.at[gb]

        def body(t, c):
            dst = pl.ds(pl.multiple_of(t * SUBLANES, SUBLANES), SUBLANES)
            for k in range(TOP_K):
                row = pl.multiple_of(slot_smem[bases[k] + t], SUBLANES)
                pltpu.make_async_copy(mo_ref.at[pl.ds(row, SUBLANES)], dsts[k].at[dst], sem).start(priority=k % 2)
            return c

        lax.fori_loop(0, tm, body, 0, unroll=ISSUE_GROUP)

    @pl.when(i == 0)
    def _():
        idx_copy(0).start()
        if n_tiles > 1:
            idx_copy(1).start()
        idx_copy(0).wait()
        issue(0)

    @pl.when(i + 1 < n_tiles)
    def _():
        idx_copy(i + 1).wait()
        issue(i + 1)

    @pl.when(i + 2 < n_tiles)
    def _():
        idx_copy(i + 2).start()

    cur = i % GATHER_RING
    for k in range(TOP_K):
        pltpu.make_async_copy(mo_ref.at[pl.ds(0, tm * SUBLANES)], gbuf.at[cur, k], sem_row.at[cur]).wait()

    g_rows = jnp.concatenate([rg_ref[...], jnp.zeros((LANES - ROUTE_ROWS, tm), F32)], axis=0)
    g = g_rows.T
    y = x1_ref[...]
    for k in range(TOP_K):
        y = y + g[:, k:k + 1] * _load_token_tiles(gbuf.at[cur, k], tm)
    out_ref[...] = _rms(y, fg_ref[...])


def _combine(dest_tiles, x1, rg, fg, mlp_out, tm):
    t, d = x1.shape
    n_tiles = t // tm
    rowd = pl.BlockSpec((tm, d), lambda i: (i, 0))
    return pl.pallas_call(
        functools.partial(_combine_kernel, tm=tm, n_tiles=n_tiles),
        grid=(n_tiles,),
        in_specs=[pl.BlockSpec(memory_space=pl.ANY), rowd,
                  pl.BlockSpec((ROUTE_ROWS, tm), lambda i: (0, i)),
                  pl.BlockSpec((1, d), lambda i: (0, 0)),
                  pl.BlockSpec(memory_space=pl.ANY)],
        out_specs=rowd,
        out_shape=jax.ShapeDtypeStruct((t, d), F32),
        scratch_shapes=[pltpu.SMEM((IDX_RING * tm * TOP_K,), I32),
                        pltpu.VMEM((GATHER_RING, TOP_K, tm * SUBLANES, LANES), F32),
                        pltpu.SemaphoreType.DMA((IDX_RING,)), pltpu.SemaphoreType.DMA((GATHER_RING,))],
        compiler_params=_cparams("arbitrary"),
        name="combine",
    )(dest_tiles, x1, rg, fg, mlp_out)


def _block_tables(counts, idx, rank, tb, nb):
    padded = (counts + tb - 1) // tb * tb
    pad_end = jnp.cumsum(padded)
    pad_start = pad_end - padded
    sel = idx[..., None] == jnp.arange(N_EXPERTS, dtype=I32)
    dest = jnp.sum(jnp.where(sel, pad_start, 0), axis=-1) + rank
    blocks_e = padded // tb
    blk_end = jnp.cumsum(blocks_e)
    total = blk_end[-1]
    bid = jnp.arange(nb, dtype=I32)
    src = jnp.minimum(bid, total - 1)
    e = jnp.minimum(jnp.sum((src[:, None] >= blk_end[None, :]).astype(I32), axis=1), N_EXPERTS - 1)
    first = jnp.sum(jnp.where(e[:, None] == jnp.arange(N_EXPERTS), blk_end - blocks_e, 0), axis=1)
    cnt = jnp.sum(jnp.where(e[:, None] == jnp.arange(N_EXPERTS), counts, 0), axis=1)
    valid = jnp.where(bid < total, jnp.clip(cnt - (src - first) * tb, 0, tb), 0)
    tail = total + jnp.arange(N_EXPERTS, dtype=I32)
    zrow = jnp.concatenate([jnp.where(padded > 0, pad_end - tb, -1), jnp.where(tail < nb, tail * tb, -1)])
    return dest.astype(I32), e.astype(I32), src.astype(I32), valid.astype(I32), zrow.astype(I32)


def _tile_major(dest, tm):
    k, t = dest.shape
    return (dest * SUBLANES).reshape(k, t // tm, tm).transpose(1, 0, 2).reshape(-1)


def kernel(x, mix_norm_g, w_in, conv_w, attn_out_norm_g, conv_out_norm_g, w_out, ffn_norm_g, w_router,
           b_router, w_gate_up, b_gate_up, w_down, b_down, final_norm_g):
    batch, seq, d = x.shape
    t = batch * seq
    assert w_in.shape[0] == 1, "single-layer trunk: the final RMSNorm is fused into the combine kernel"
    assert d == SUBLANES * LANES, "one token row must fill exactly one (8, 128) f32 tile"
    tm = 512
    tb = 512
    tc = 512
    td = 1024
    nb = (t * TOP_K) // tb + N_EXPERTS
    x2 = x.reshape(t, d)

    *qkv, gb, z = _proj(x2, mix_norm_g[0][None, :], w_in[0].astype(BF16), tm, batch, seq)
    o_views, l_views = [], []
    for di, dil in enumerate(DILATIONS):
        o, lse = _attn_branch(qkv[di], qkv[3 + di], qkv[6 + di], dil)
        o_views.append(o)
        l_views.append(lse)

    x1, h2, ri, rg, cnt = _mix(x2, o_views, l_views, gb, z, conv_w[0], attn_out_norm_g[0][None, :],
                               conv_out_norm_g[0][None, :], w_out[0].astype(BF16), ffn_norm_g[0][None, :],
                               w_router[0], b_router[0], tm, batch, seq)

    counts = cnt[:, 0].astype(I32)
    dest, blk_e, blk_x, blk_v, zrow = _block_tables(counts, ri[:TOP_K], ri[TOP_K:], tb, nb)

    xs = _dispatch(zrow, _tile_major(dest, td), h2, nb * tb, td, tb)
    mo = _experts(blk_e, blk_x, blk_v, xs, w_gate_up[0], b_gate_up[0][:, None, :],
                  w_down[0], b_down[0][:, None, :], tb)
    out = _combine(_tile_major(dest, tc), x1, rg, final_norm_g[None, :], mo, tc)
    return out.reshape(batch, seq, d)
```
